```python
import math
import jax, jax.numpy as jnp
from jax import lax
import numpy as np

D_MODEL = 2048
BATCH = 1
SEQ = 8192
DEPTH = 2

N_META = 16
D_MIX = 2 * D_MODEL
EPS = 1e-6
POOL_WIDTH = D_MIX // 4
POOL_WINDOWS = (2, 4, 8, 16)
POOL_GROUP = POOL_WIDTH // 4
SSD_WIDTH = D_MIX // 2
SSD_HEADDIM = 64
SSD_HEADS = SSD_WIDTH // SSD_HEADDIM
SSD_GROUPS = 4
SSD_STATE = 128
SSD_CONV = 4
SSD_CHUNK = 128
SSD_CONV_DIM = SSD_WIDTH + 2 * SSD_GROUPS * SSD_STATE
ATTN_WIDTH = D_MIX // 4
ATTN_HEADDIM = 128
ATTN_HEADS = ATTN_WIDTH // ATTN_HEADDIM
ATTN_KV_HEADS = 2
IDX_HEADS = 16
IDX_HEADDIM = 64
INDEX_TOPK = 256
Q_BLOCK = 128
ROPE_THETA = 500000.0
ROPE_FRACTION = 4

SPLIT_SIZES = (POOL_WIDTH, POOL_WIDTH,
               SSD_WIDTH, SSD_CONV_DIM, SSD_HEADS,
               ATTN_HEADS * ATTN_HEADDIM,
               ATTN_KV_HEADS * ATTN_HEADDIM,
               ATTN_KV_HEADS * ATTN_HEADDIM,
               ATTN_WIDTH,
               IDX_HEADS * IDX_HEADDIM, IDX_HEADDIM, IDX_HEADS)
D_IN = sum(SPLIT_SIZES)

kernel_name = 'hybrid_pool_ssd_dsa_block'


def rmsnorm(x, w):
    xf = x.astype(jnp.float32)
    y = xf * lax.rsqrt(jnp.mean(xf * xf, axis=-1, keepdims=True) + EPS)
    return (y * w.astype(jnp.float32)).astype(x.dtype)


def rope_partial(x, pos):
    d = x.shape[-1]
    rot = d // ROPE_FRACTION
    half = rot // 2
    inv = jnp.power(jnp.float32(ROPE_THETA), -(jnp.arange(half, dtype=jnp.float32) * 2.0 / rot))
    ang = pos.astype(jnp.float32)[:, None] * inv[None, :]
    cos = jnp.cos(ang)[:, None, :]
    sin = jnp.sin(ang)[:, None, :]
    xf = x.astype(jnp.float32)
    x1 = xf[..., :half]
    x2 = xf[..., half:rot]
    out = jnp.concatenate([x1 * cos - x2 * sin, x2 * cos + x1 * sin, xf[..., rot:]], axis=-1)
    return out.astype(x.dtype)


def pool_mixer(v, pool_w, pool_scale):
    b, t, _ = v.shape
    vf = v.astype(jnp.float32)
    cs0 = jnp.pad(jnp.cumsum(vf, axis=1), ((0, 0), (1, 0), (0, 0)))
    hi = jnp.arange(1, t + 1)
    groups = []
    for g, w in enumerate(POOL_WINDOWS):
        lo = jnp.maximum(hi - w, 0)
        cnt = (hi - lo).astype(jnp.float32)[None, :, None]
        sl = slice(g * POOL_GROUP, (g + 1) * POOL_GROUP)
        groups.append((cs0[:, hi, sl] - cs0[:, lo, sl]) / cnt)
    pooled = jnp.stack(groups, axis=2)
    y = (pooled - vf.reshape(b, t, len(POOL_WINDOWS), POOL_GROUP)).astype(v.dtype)
    y = jnp.einsum('btgc,gcd->btgd', y, pool_w).reshape(b, t, POOL_WIDTH)
    return y * pool_scale


def causal_dwconv(x, w, bias):
    k = w.shape[0]
    t = x.shape[1]
    xp = jnp.pad(x, ((0, 0), (k - 1, 0), (0, 0)))
    out = bias
    for i in range(k):
        out = out + xp[:, i:i + t] * w[i]
    return out


def segsum_exp(a):
    l = a.shape[-1]
    cs = jnp.cumsum(a, axis=-1)
    diff = cs[..., :, None] - cs[..., None, :]
    mask = jnp.tril(jnp.ones((l, l), dtype=bool))
    return jnp.where(mask, jnp.exp(jnp.where(mask, diff, 0.0)), 0.0)


def ssd_chunked(x, dt, a, bm, cm):
    b, tp, h, p = x.shape
    g, n = bm.shape[2], bm.shape[3]
    r = h // g
    c = tp // SSD_CHUNK
    l = SSD_CHUNK
    x = x.reshape(b, c, l, g, r, p)
    dt = dt.reshape(b, c, l, g, r)
    bm = bm.reshape(b, c, l, g, n)
    cm = cm.reshape(b, c, l, g, n)
    xdt = x * dt[..., None]
    da = dt.transpose(0, 1, 3, 4, 2) * a.reshape(g, r)[None, None, :, :, None]
    cs = jnp.cumsum(da, axis=-1)
    lm = segsum_exp(da)
    cb = jnp.einsum('bclgn,bcsgn->bcgls', cm, bm)
    y_diag = jnp.einsum('bcgrls,bcsgrp->bclgrp', cb[:, :, :, None] * lm, xdt)
    decay_states = jnp.exp(cs[..., -1:] - cs)
    states = jnp.einsum('bcsgn,bcgrs,bcsgrp->bcgrpn', bm, decay_states, xdt)
    chunk_decay = jnp.exp(cs[..., -1])

    def step(hstate, inp):
        dec, st = inp
        return hstate * dec[..., None, None] + st, hstate

    h0 = jnp.zeros((b, g, r, p, n), jnp.float32)
    _, prev = lax.scan(step, h0, (jnp.swapaxes(chunk_decay, 0, 1), jnp.swapaxes(states, 0, 1)))
    prev = jnp.swapaxes(prev, 0, 1)
    y_off = jnp.einsum('bclgn,bcgrpn,bcgrl->bclgrp', cm, prev, jnp.exp(cs))
    return (y_diag + y_off).reshape(b, tp, h, p)


def pad_front(a, n):
    return jnp.pad(a, ((0, 0), (n, 0)) + ((0, 0),) * (a.ndim - 2))


def ssd_mixer(xbc_raw, z, dt_raw, conv_w, conv_b, dt_bias, a_log, d_skip, norm_w):
    b, t, _ = xbc_raw.shape
    xbc = jax.nn.silu(causal_dwconv(xbc_raw, conv_w, conv_b)).astype(jnp.float32)
    xs = xbc[..., :SSD_WIDTH].reshape(b, t, SSD_HEADS, SSD_HEADDIM)
    bm = xbc[..., SSD_WIDTH:SSD_WIDTH + SSD_GROUPS * SSD_STATE].reshape(b, t, SSD_GROUPS, SSD_STATE)
    cm = xbc[..., SSD_WIDTH + SSD_GROUPS * SSD_STATE:].reshape(b, t, SSD_GROUPS, SSD_STATE)
    dt = jax.nn.softplus(dt_raw.astype(jnp.float32) + dt_bias.astype(jnp.float32))
    a = -jnp.exp(a_log.astype(jnp.float32))
    pad = (SSD_CHUNK - N_META % SSD_CHUNK) % SSD_CHUNK
    y = ssd_chunked(pad_front(xs, pad), pad_front(dt, pad), a,
                    pad_front(bm, pad), pad_front(cm, pad))[:, pad:]
    y = y + d_skip.astype(jnp.float32)[:, None] * xs
    yg = (y.reshape(b, t, SSD_WIDTH) * jax.nn.silu(z.astype(jnp.float32))).reshape(b, t, SSD_GROUPS, -1)
    yg = yg * lax.rsqrt(jnp.mean(yg * yg, axis=-1, keepdims=True) + EPS)
    return (yg.reshape(b, t, SSD_WIDTH) * norm_w.astype(jnp.float32)).astype(xbc_raw.dtype)


def dsa_attention(q, k, v, q_idx, k_idx, w_idx, topk):
    b, t, h, d = q.shape
    kv = k.shape[2]
    rep = h // kv
    nb = -(-t // Q_BLOCK)
    tq = nb * Q_BLOCK
    scale = d ** -0.5
    idx_scale = (IDX_HEADDIM ** -0.5) * (IDX_HEADS ** -0.5)

    def blocks(a):
        a = jnp.pad(a, ((0, 0), (0, tq - t)) + ((0, 0),) * (a.ndim - 2))
        return jnp.swapaxes(a.reshape((b, nb, Q_BLOCK) + a.shape[2:]), 0, 1)

    key_pos = jnp.arange(t)
    k_idx_f = k_idx.astype(jnp.float32)
    gather = jax.vmap(lambda arr, ix: arr[ix])

    def one_block(args):
        i, qi, qii, wi = args
        q_pos = i * Q_BLOCK + jnp.arange(Q_BLOCK)
        s = jax.nn.relu(jnp.einsum('bqhd,bsd->bqhs', qii.astype(jnp.float32), k_idx_f))
        score = jnp.einsum('bqhs,bqh->bqs', s, wi.astype(jnp.float32)) * idx_scale
        causal = key_pos[None, :] <= q_pos[:, None]
        score = jnp.where(causal[None], score, -jnp.inf)
        _, sel = lax.top_k(score, topk)
        valid = sel <= q_pos[None, :, None]
        ks = gather(k, sel).astype(jnp.float32)
        vs = gather(v, sel).astype(jnp.float32)
        qg = qi.reshape(b, Q_BLOCK, kv, rep, d).astype(jnp.float32)
        logits = jnp.einsum('bqgrd,bqkgd->bqgrk', qg, ks) * scale
        logits = jnp.where(valid[:, :, None, None, :], logits, -jnp.inf)
        p = jax.nn.softmax(logits, axis=-1)
        o = jnp.einsum('bqgrk,bqkgd->bqgrd', p, vs)
        return o.reshape(b, Q_BLOCK, h, d).astype(q.dtype)

    out = lax.map(one_block, (jnp.arange(nb), blocks(q), blocks(q_idx), blocks(w_idx)))
    return jnp.swapaxes(out, 0, 1).reshape(b, tq, h, d)[:, :t]


def hybrid_layer(h, pos, topk, pre_w, post_w, w_in, pool_w, pool_scale, conv_w, conv_b,
                 dt_bias, a_log, d_skip, ssd_norm_w, w_out):
    b, t, _ = h.shape
    u = rmsnorm(h, pre_w)
    proj = jnp.einsum('btd,de->bte', u, w_in)
    offsets = np.cumsum(SPLIT_SIZES)[:-1].tolist()
    (pool_v, pool_g, ssd_z, ssd_xbc, ssd_dt, a_q, a_k, a_v, a_g,
     i_q, i_k, i_w) = jnp.split(proj, offsets, axis=-1)
    pool_out = jax.nn.silu(pool_g) * pool_mixer(pool_v, pool_w, pool_scale)
    ssd_out = ssd_mixer(ssd_xbc, ssd_z, ssd_dt, conv_w, conv_b, dt_bias, a_log, d_skip, ssd_norm_w)
    q = rope_partial(a_q.reshape(b, t, ATTN_HEADS, ATTN_HEADDIM), pos)
    k = rope_partial(a_k.reshape(b, t, ATTN_KV_HEADS, ATTN_HEADDIM), pos)
    v = a_v.reshape(b, t, ATTN_KV_HEADS, ATTN_HEADDIM)
    qi = rope_partial(i_q.reshape(b, t, IDX_HEADS, IDX_HEADDIM), pos)
    ki = rope_partial(i_k.reshape(b, t, 1, IDX_HEADDIM), pos)[:, :, 0]
    attn = dsa_attention(q, k, v, qi, ki, i_w, topk).reshape(b, t, ATTN_WIDTH)
    attn_out = jax.nn.silu(a_g) * attn
    mix = jnp.concatenate([pool_out, ssd_out, attn_out], axis=-1)
    out = jnp.einsum('bte,ed->btd', mix, w_out)
    return h + rmsnorm(out, post_w)


def setup_inputs(seed: int = 0) -> dict:
    key = jax.random.key(seed)
    ks = jax.random.split(key, 14)
    f = jnp.float32
    x = jax.random.normal(ks[0], (BATCH, SEQ, D_MODEL), f)
    meta_tokens = jax.random.normal(ks[1], (N_META, D_MODEL), f)
    pre_norm_w = 1.0 + 0.1 * jax.random.normal(ks[2], (DEPTH, D_MODEL), f)
    post_norm_w = 1.0 + 0.1 * jax.random.normal(ks[3], (DEPTH, D_MODEL), f)
    w_in = jax.random.normal(ks[4], (DEPTH, D_MODEL, D_IN), f) * D_MODEL ** -0.5
    pool_w = jax.random.normal(ks[5], (DEPTH, len(POOL_WINDOWS), POOL_GROUP, POOL_GROUP), f) * POOL_GROUP ** -0.5
    pool_scale = 1.0 + 0.1 * jax.random.normal(ks[6], (DEPTH, POOL_WIDTH), f)
    conv_w = jax.random.normal(ks[7], (DEPTH, SSD_CONV, SSD_CONV_DIM), f) * SSD_CONV ** -0.5
    conv_b = 0.01 * jax.random.normal(ks[8], (DEPTH, SSD_CONV_DIM), f)
    dt0 = jnp.exp(jax.random.uniform(ks[9], (DEPTH, SSD_HEADS), f)
                  * (math.log(0.1) - math.log(0.001)) + math.log(0.001))
    dt_bias = dt0 + jnp.log(-jnp.expm1(-dt0))
    a_log = jnp.log(jax.random.uniform(ks[10], (DEPTH, SSD_HEADS), f, minval=1.0, maxval=16.0))
    d_skip = 1.0 + 0.1 * jax.random.normal(ks[11], (DEPTH, SSD_HEADS), f)
    ssd_norm_w = 1.0 + 0.1 * jax.random.normal(ks[12], (DEPTH, SSD_WIDTH), f)
    w_out = jax.random.normal(ks[13], (DEPTH, D_MIX, D_MODEL), f) * D_MIX ** -0.5
    return {'x': x, 'meta_tokens': meta_tokens, 'pre_norm_w': pre_norm_w, 'post_norm_w': post_norm_w,
            'w_in': w_in, 'pool_w': pool_w, 'pool_scale': pool_scale, 'conv_w': conv_w, 'conv_b': conv_b,
            'dt_bias': dt_bias, 'a_log': a_log, 'd_skip': d_skip, 'ssd_norm_w': ssd_norm_w, 'w_out': w_out}


def reference(x, meta_tokens, pre_norm_w, post_norm_w, w_in, pool_w, pool_scale, conv_w, conv_b,
              dt_bias, a_log, d_skip, ssd_norm_w, w_out):
    b, s, d = x.shape
    meta = jnp.broadcast_to(meta_tokens.astype(x.dtype)[None], (b, N_META, d))
    h = jnp.concatenate([meta, x], axis=1)
    pos = jnp.arange(s + N_META)
    topk = min(INDEX_TOPK, s // 4)
    for l in range(DEPTH):
        h = hybrid_layer(h, pos, topk, pre_norm_w[l], post_norm_w[l], w_in[l], pool_w[l], pool_scale[l],
                         conv_w[l], conv_b[l], dt_bias[l], a_log[l], d_skip[l], ssd_norm_w[l], w_out[l])
    return h[:, N_META:]
```

```python
import functools
import math

import jax
import jax.numpy as jnp
import numpy as np
from jax import lax
from jax.experimental import pallas as pl
from jax.experimental.pallas import tpu as pltpu

f32 = jnp.float32
bf16 = jnp.bfloat16
MXU_DTYPE = bf16

D_MODEL = 2048
N_META = 16
EPS = 1e-6
POOL_WIDTH = 1024
POOL_WINDOWS = (2, 4, 8, 16)
POOL_GROUP = 256
SSD_WIDTH = 2048
SSD_HEADDIM = 64
SSD_HEADS = 32
SSD_GROUPS = 4
SSD_STATE = 128
SSD_CONV = 4
SSD_CHUNK = 128
SSD_BC = 2 * SSD_GROUPS * SSD_STATE
ATTN_WIDTH = 1024
ATTN_HEADDIM = 128
ATTN_HEADS = 8
ATTN_KV_HEADS = 2
ATTN_REP = ATTN_HEADS // ATTN_KV_HEADS
IDX_HEADS = 16
IDX_HEADDIM = 64
INDEX_TOPK = 256
ROPE_THETA = 500000.0
ROPE_FRACTION = 4
D_MIX = 4096

LANES = 128
VMEM_LIMIT_BYTES = 56 * 1024 * 1024

FRONT_PAD = (SSD_CHUNK - N_META % SSD_CHUNK) % SSD_CHUNK
ROW0 = FRONT_PAD + N_META

C_POOL_V = 0
C_POOL_G = 1024
C_Z = 2048
C_XS = 4096
C_BC = 6144
C_Q = 7168
C_AG = 8192
C_IQ = 9216
C_KV = 10240
C_MISC = 10752
N_PACK = 10880
MISC_IW = IDX_HEADDIM
MISC_DT = IDX_HEADDIM + IDX_HEADS

INT_MIN = -(2 ** 31)
NEG_BIG = -1e30


def _tiles(tp):
    assert tp % 640 == 0, tp
    return dict(tm=640, tn=640, tq=128, tk=640)


def _dot(a, b):
    return jnp.dot(a.astype(MXU_DTYPE), b.astype(MXU_DTYPE), preferred_element_type=f32)


def _dot_nt(a, b):
    return lax.dot_general(a.astype(MXU_DTYPE), b.astype(MXU_DTYPE),
                           (((1,), (1,)), ((), ())), preferred_element_type=f32)


def _silu(x):
    return x * (1.0 / (1.0 + jnp.exp(-x)))


def _inproj_kernel(h_ref, w_norm_ref, w_ref, o_ref, u_ref):
    @pl.when(pl.program_id(1) == 0)
    def _():
        x = h_ref[...]
        y = x * lax.rsqrt(jnp.mean(x * x, axis=-1, keepdims=True) + EPS)
        u_ref[...] = (y * w_norm_ref[...]).astype(u_ref.dtype)

    o_ref[...] = jnp.dot(u_ref[...], w_ref[...], preferred_element_type=f32)


def _inproj(h, pre_w, w_pack, tm, tn):
    tp = h.shape[0]
    return pl.pallas_call(
        _inproj_kernel,
        grid=(tp // tm, N_PACK // tn),
        in_specs=[pl.BlockSpec((tm, D_MODEL), lambda i, j: (i, 0)),
                  pl.BlockSpec((1, D_MODEL), lambda i, j: (0, 0)),
                  pl.BlockSpec((D_MODEL, tn), lambda i, j: (0, j))],
        out_specs=pl.BlockSpec((tm, tn), lambda i, j: (i, j)),
        out_shape=jax.ShapeDtypeStruct((tp, N_PACK), f32),
        scratch_shapes=[pltpu.VMEM((tm, D_MODEL), MXU_DTYPE)],
        compiler_params=pltpu.CompilerParams(
            dimension_semantics=("arbitrary", "arbitrary"), vmem_limit_bytes=VMEM_LIMIT_BYTES),
        name="inproj",
    )(h, pre_w, w_pack)


def _rope_tile(x, cos, sin, half, lane):
    up = pltpu.roll(x, LANES - half, 1)
    down = pltpu.roll(x, half, 1)
    return x * cos + jnp.where(lane, up, down) * sin


def _rope_kernel(q_ref, iq_ref, kv_ref, misc_ref, cos_a_ref, sin_a_ref, cos_i_ref, sin_i_ref,
                 qo_ref, iqo_ref, kvo_ref, ika_ref, ikb_ref):
    rows = q_ref.shape[0]
    lane_id = lax.broadcasted_iota(jnp.int32, (rows, LANES), 1)
    half_a = ATTN_HEADDIM // ROPE_FRACTION // 2
    half_i = IDX_HEADDIM // ROPE_FRACTION // 2
    first_a = lane_id < half_a
    first_i = (lane_id % IDX_HEADDIM) < half_i
    cos_a, sin_a = cos_a_ref[...], sin_a_ref[...]
    cos_i, sin_i = cos_i_ref[...], sin_i_ref[...]
    q_scale = ATTN_HEADDIM ** -0.5
    for hd in range(ATTN_HEADS):
        sl = slice(hd * LANES, (hd + 1) * LANES)
        qo_ref[:, sl] = (_rope_tile(q_ref[:, sl], cos_a, sin_a, half_a, first_a) * q_scale
                         ).astype(qo_ref.dtype)
    for hd in range(ATTN_KV_HEADS):
        sl = slice(hd * LANES, (hd + 1) * LANES)
        kvo_ref[:, sl] = _rope_tile(kv_ref[:, sl], cos_a, sin_a, half_a, first_a).astype(kvo_ref.dtype)
    v_sl = slice(ATTN_KV_HEADS * LANES, 2 * ATTN_KV_HEADS * LANES)
    kvo_ref[:, v_sl] = kv_ref[:, v_sl].astype(kvo_ref.dtype)
    for pair in range(IDX_HEADS * IDX_HEADDIM // LANES):
        sl = slice(pair * LANES, (pair + 1) * LANES)
        iqo_ref[:, sl] = _rope_tile(iq_ref[:, sl], cos_i, sin_i, half_i, first_i).astype(iqo_ref.dtype)
    ik = _rope_tile(misc_ref[...], cos_i, sin_i, half_i, first_i)
    low = lane_id < IDX_HEADDIM
    ika_ref[...] = jnp.where(low, ik, 0.0).astype(ika_ref.dtype)
    ikb_ref[...] = jnp.where(low, 0.0, pltpu.roll(ik, IDX_HEADDIM, 1)).astype(ikb_ref.dtype)


def _rope(proj, tabs, tm):
    tp = proj.shape[0]
    wide = lambda c: pl.BlockSpec((tm, 1024), lambda i, c=c: (i, c // 1024))
    tab = pl.BlockSpec((tm, LANES), lambda i: (i, 0))
    return pl.pallas_call(
        _rope_kernel,
        grid=(tp // tm,),
        in_specs=[wide(C_Q), wide(C_IQ),
                  pl.BlockSpec((tm, 512), lambda i: (i, C_KV // 512)),
                  pl.BlockSpec((tm, LANES), lambda i: (i, C_MISC // LANES)),
                  tab, tab, tab, tab],
        out_specs=[pl.BlockSpec((tm, 1024), lambda i: (i, 0)),
                   pl.BlockSpec((tm, 1024), lambda i: (i, 0)),
                   pl.BlockSpec((tm, 512), lambda i: (i, 0)),
                   tab, tab],
        out_shape=[jax.ShapeDtypeStruct((tp, 1024), MXU_DTYPE),
                   jax.ShapeDtypeStruct((tp, 1024), MXU_DTYPE),
                   jax.ShapeDtypeStruct((tp, 512), MXU_DTYPE),
                   jax.ShapeDtypeStruct((tp, LANES), MXU_DTYPE),
                   jax.ShapeDtypeStruct((tp, LANES), MXU_DTYPE)],
        compiler_params=pltpu.CompilerParams(
            dimension_semantics=("arbitrary",), vmem_limit_bytes=VMEM_LIMIT_BYTES),
        name="rope",
    )(proj, proj, proj, proj, *tabs)


def _rope_tables(tp):
    pos = jnp.arange(tp) - FRONT_PAD

    def one(head_dim):
        rot = head_dim // ROPE_FRACTION
        half = rot // 2
        inv = jnp.power(jnp.float32(ROPE_THETA), -(jnp.arange(half, dtype=jnp.float32) * 2.0 / rot))
        ang = pos.astype(jnp.float32)[:, None] * inv[None, :]
        cos, sin = jnp.cos(ang), jnp.sin(ang)
        rest = head_dim - rot
        cos_p = jnp.concatenate([cos, cos, jnp.ones((tp, rest), f32)], axis=1)
        sin_p = jnp.concatenate([-sin, sin, jnp.zeros((tp, rest), f32)], axis=1)
        reps = LANES // head_dim
        return jnp.tile(cos_p, (1, reps)), jnp.tile(sin_p, (1, reps))

    cos_a, sin_a = one(ATTN_HEADDIM)
    cos_i, sin_i = one(IDX_HEADDIM)
    return cos_a, sin_a, cos_i, sin_i


POOL_HALO = 16
CONV_HALO = 8


def _mixer_kernel(pv_ref, pg_ref, z_ref, xs_ref, bc_ref, misc_ref,
                  pool_w_ref, pool_scale_ref, conv_w_ref, conv_b_ref, dt_bias_ref, a_log_ref,
                  d_skip_ref, norm_w_ref,
                  out_ref,
                  pbuf, xbuf, state, ybuf):
    c = pl.program_id(0)
    L = SSD_CHUNK

    @pl.when(c == 0)
    def _():
        pbuf[0:POOL_HALO, :] = jnp.zeros((POOL_HALO, POOL_WIDTH), f32)
        xbuf[0:CONV_HALO, :] = jnp.zeros((CONV_HALO, SSD_WIDTH + SSD_BC), f32)
        state[...] = jnp.zeros_like(state)

    row = c * L + lax.broadcasted_iota(jnp.int32, (L, 1), 0)
    real = row >= FRONT_PAD
    tpos = row - FRONT_PAD

    pbuf[POOL_HALO:POOL_HALO + L, :] = pv_ref[...]
    for g, w in enumerate(POOL_WINDOWS):
        sl = slice(g * POOL_GROUP, (g + 1) * POOL_GROUP)
        v = pbuf[POOL_HALO:POOL_HALO + L, sl]
        acc = v
        for j in range(1, w):
            acc = acc + pbuf[POOL_HALO - j:POOL_HALO - j + L, sl]
        cnt = jnp.clip(tpos + 1, 1, w).astype(f32)
        y = acc / cnt - v
        y = _dot(y, pool_w_ref[g]) * pool_scale_ref[:, sl]
        out_ref[:, sl] = (_silu(pg_ref[:, sl]) * y).astype(out_ref.dtype)
    pbuf[0:POOL_HALO, :] = pbuf[L:L + POOL_HALO, :]

    xbuf[CONV_HALO:CONV_HALO + L, 0:SSD_WIDTH] = xs_ref[...]
    xbuf[CONV_HALO:CONV_HALO + L, SSD_WIDTH:] = bc_ref[...]

    def conv(sl):
        acc = conv_b_ref[:, sl] + xbuf[CONV_HALO:CONV_HALO + L, sl] * conv_w_ref[SSD_CONV - 1:SSD_CONV, sl]
        for j in range(1, SSD_CONV):
            acc = acc + (xbuf[CONV_HALO - j:CONV_HALO - j + L, sl]
                         * conv_w_ref[SSD_CONV - 1 - j:SSD_CONV - j, sl])
        return jnp.where(real, _silu(acc), 0.0)

    dt_in = misc_ref[...] + dt_bias_ref[...]
    dt = jnp.maximum(dt_in, 0.0) + jnp.log1p(jnp.exp(-jnp.abs(dt_in)))
    dt = jnp.where(real, dt, 0.0)
    da = dt * (-jnp.exp(a_log_ref[...]))
    r_id = lax.broadcasted_iota(jnp.int32, (L, L), 0)
    c_id = lax.broadcasted_iota(jnp.int32, (L, L), 1)
    tril = r_id >= c_id
    cs = jnp.dot(tril.astype(f32), da, preferred_element_type=f32,
                 precision=lax.Precision.HIGHEST)
    cs_t = cs.T
    dt_t = dt.T

    for g in range(SSD_GROUPS):
        b_sl = slice(SSD_WIDTH + g * SSD_STATE, SSD_WIDTH + (g + 1) * SSD_STATE)
        c_sl = slice(SSD_WIDTH + SSD_GROUPS * SSD_STATE + g * SSD_STATE,
                     SSD_WIDTH + SSD_GROUPS * SSD_STATE + (g + 1) * SSD_STATE)
        bm = conv(b_sl)
        cm = conv(c_sl)
        cb = _dot_nt(cm, bm)
        bm_t = bm.T
        cm_lo = cm.astype(MXU_DTYPE)
        for r in range(SSD_HEADS // SSD_GROUPS):
            hd = g * (SSD_HEADS // SSD_GROUPS) + r
            lane = MISC_DT + hd
            h_sl = slice(hd * SSD_HEADDIM, (hd + 1) * SSD_HEADDIM)
            xs = conv(h_sl)
            xs_lo = xs.astype(MXU_DTYPE)
            cs_col = jnp.broadcast_to(cs[:, lane:lane + 1], (L, L))
            cs_row = cs_t[lane:lane + 1, :]
            dt_row = dt_t[lane:lane + 1, :]
            decay = jnp.exp(jnp.where(tril, cs_col - cs_row, NEG_BIG))
            y = _dot(cb * decay * dt_row, xs_lo)
            prev = state[:, h_sl]
            y = y + _dot(cm_lo, prev) * jnp.exp(cs_col[:, 0:SSD_HEADDIM])
            ybuf[:, h_sl] = y + d_skip_ref[:, h_sl] * xs
            cs_last = cs_row[:, L - 1:L]
            w_s = jnp.exp(cs_last - cs_row) * dt_row
            state[:, h_sl] = prev * jnp.exp(cs_last) + _dot(bm_t * w_s, xs_lo)

    xbuf[0:CONV_HALO, :] = xbuf[L:L + CONV_HALO, :]

    gw = SSD_WIDTH // SSD_GROUPS
    for g in range(SSD_GROUPS):
        sl = slice(g * gw, (g + 1) * gw)
        yg = ybuf[:, sl] * _silu(z_ref[:, sl])
        yg = yg * lax.rsqrt(jnp.mean(yg * yg, axis=-1, keepdims=True) + EPS)
        out_ref[:, POOL_WIDTH + g * gw:POOL_WIDTH + (g + 1) * gw] = (
            yg * norm_w_ref[:, sl]).astype(out_ref.dtype)


def _mixer(proj, pool_w, pool_scale, conv_w, conv_b, dt_bias_row, a_log_row, d_skip_wide, norm_w):
    tp = proj.shape[0]
    L = SSD_CHUNK
    col = lambda width, c: pl.BlockSpec((L, width), lambda i, c=c, width=width: (i, c // width))
    full = lambda shape: pl.BlockSpec(shape, lambda i, n=len(shape): (0,) * n)
    return pl.pallas_call(
        _mixer_kernel,
        grid=(tp // L,),
        in_specs=[col(1024, C_POOL_V), col(1024, C_POOL_G), col(2048, C_Z), col(2048, C_XS),
                  col(1024, C_BC), col(LANES, C_MISC),
                  full((len(POOL_WINDOWS), POOL_GROUP, POOL_GROUP)), full((1, POOL_WIDTH)),
                  full((SSD_CONV, SSD_WIDTH + SSD_BC)), full((1, SSD_WIDTH + SSD_BC)),
                  full((1, LANES)), full((1, LANES)), full((1, SSD_WIDTH)), full((1, SSD_WIDTH))],
        out_specs=pl.BlockSpec((L, POOL_WIDTH + SSD_WIDTH), lambda i: (i, 0)),
        out_shape=jax.ShapeDtypeStruct((tp, POOL_WIDTH + SSD_WIDTH), MXU_DTYPE),
        scratch_shapes=[pltpu.VMEM((POOL_HALO + L, POOL_WIDTH), f32),
                        pltpu.VMEM((CONV_HALO + L, SSD_WIDTH + SSD_BC), f32),
                        pltpu.VMEM((SSD_STATE, SSD_WIDTH), f32),
                        pltpu.VMEM((L, SSD_WIDTH), f32)],
        compiler_params=pltpu.CompilerParams(
            dimension_semantics=("arbitrary",), vmem_limit_bytes=VMEM_LIMIT_BYTES),
        name="mixer",
    )(proj, proj, proj, proj, proj, proj, pool_w, pool_scale, conv_w, conv_b, dt_bias_row, a_log_row,
      d_skip_wide, norm_w)


def _attn_kernel(q_ref, iq_ref, misc_ref, ag_ref, kv_ref, ika_ref, ikb_ref,
                 out_ref,
                 key_ref, m_ref, l_ref, acc_ref, *, topk):
    qb = pl.program_id(0)
    tq = q_ref.shape[0]
    tk = kv_ref.shape[1]
    q0 = qb * tq
    nkb = (q0 + tq - 1) // tk + 1
    idx_scale = (IDX_HEADDIM ** -0.5) * (IDX_HEADS ** -0.5)

    row = q0 + lax.broadcasted_iota(jnp.int32, (tq, tk), 0)
    col_in_blk = lax.broadcasted_iota(jnp.int32, (tq, tk), 1)
    w_idx = misc_ref[...] * idx_scale

    def score_block(kb, carry):
        ka = ika_ref[kb]
        kb_ = ikb_ref[kb]
        acc = jnp.zeros((tq, tk), f32)
        for pair in range(IDX_HEADS // 2):
            qp = iq_ref[:, pair * LANES:(pair + 1) * LANES]
            w0 = w_idx[:, MISC_IW + 2 * pair:MISC_IW + 2 * pair + 1]
            w1 = w_idx[:, MISC_IW + 2 * pair + 1:MISC_IW + 2 * pair + 2]
            acc = acc + jnp.maximum(_dot_nt(qp, ka), 0.0) * w0
            acc = acc + jnp.maximum(_dot_nt(qp, kb_), 0.0) * w1
        bits = lax.bitcast_convert_type(acc, jnp.int32)
        key = jnp.where(bits < 0, bits ^ jnp.int32(0x7FFFFFFF), bits)
        col = kb * tk + col_in_blk
        valid = (col <= row) & (col >= FRONT_PAD)
        key_ref[kb] = jnp.where(valid, key, jnp.int32(INT_MIN))
        return carry

    lax.fori_loop(0, nkb, score_block, 0)

    def count_ge(cand):
        def body(kb, cnt):
            blk = key_ref[kb]
            for j in range(tk // LANES):
                cnt = cnt + (blk[:, j * LANES:(j + 1) * LANES] >= cand).astype(jnp.int32)
            return cnt
        cnt = lax.fori_loop(0, nkb, body, jnp.zeros((tq, LANES), jnp.int32))
        return jnp.sum(cnt, axis=1, keepdims=True)

    zero = jnp.zeros((tq, 1), jnp.int32)
    prefix = jnp.where(count_ge(zero) >= topk, zero, jnp.int32(INT_MIN))

    def bit_step(it, prefix):
        cand = prefix + jnp.left_shift(jnp.int32(1), 30 - it)
        return jnp.where(count_ge(cand) >= topk, cand, prefix)

    prefix = lax.fori_loop(0, 31, bit_step, prefix)
    thr = jnp.maximum(prefix, jnp.int32(INT_MIN + 1))

    m_ref[...] = jnp.full_like(m_ref, NEG_BIG)
    l_ref[...] = jnp.zeros_like(l_ref)
    acc_ref[...] = jnp.zeros_like(acc_ref)
    q_stack = [jnp.concatenate([q_ref[:, (g * ATTN_REP + r) * LANES:(g * ATTN_REP + r + 1) * LANES]
                                for r in range(ATTN_REP)], axis=0)
               for g in range(ATTN_KV_HEADS)]

    def attend(kb, carry):
        bias = jnp.where(key_ref[kb] >= thr, 0.0, NEG_BIG)
        bias = jnp.concatenate([bias] * ATTN_REP, axis=0)
        kv = kv_ref[kb]
        for g in range(ATTN_KV_HEADS):
            k_g = kv[:, g * LANES:(g + 1) * LANES]
            v_g = kv[:, (ATTN_KV_HEADS + g) * LANES:(ATTN_KV_HEADS + g + 1) * LANES]
            s = _dot_nt(q_stack[g], k_g) + bias
            m_old = m_ref[g]
            m_new = jnp.maximum(m_old, jnp.max(s, axis=1, keepdims=True))
            alpha = jnp.exp(m_old - m_new)
            p = jnp.exp(s - m_new)
            l_ref[g] = alpha * l_ref[g] + jnp.sum(p, axis=1, keepdims=True)
            acc_ref[g] = alpha * acc_ref[g] + _dot(p, v_g)
            m_ref[g] = m_new
        return carry

    lax.fori_loop(0, nkb, attend, 0)

    for g in range(ATTN_KV_HEADS):
        o = acc_ref[g] / l_ref[g]
        for r in range(ATTN_REP):
            sl = slice((g * ATTN_REP + r) * LANES, (g * ATTN_REP + r + 1) * LANES)
            out_ref[:, sl] = (_silu(ag_ref[:, sl]) * o[r * tq:(r + 1) * tq, :]).astype(out_ref.dtype)


def _attention(proj, q_r, iq_r, kv_r, ik_a, ik_b, topk, tq, tk):
    tp = proj.shape[0]
    nk = tp // tk
    kv3 = kv_r.reshape(nk, tk, kv_r.shape[1])
    ika3 = ik_a.reshape(nk, tk, LANES)
    ikb3 = ik_b.reshape(nk, tk, LANES)
    res = lambda a: pl.BlockSpec(a.shape, lambda i: (0, 0, 0))
    return pl.pallas_call(
        functools.partial(_attn_kernel, topk=topk),
        grid=(tp // tq,),
        in_specs=[pl.BlockSpec((tq, 1024), lambda i: (i, 0)),
                  pl.BlockSpec((tq, 1024), lambda i: (i, 0)),
                  pl.BlockSpec((tq, LANES), lambda i: (i, C_MISC // LANES)),
                  pl.BlockSpec((tq, 1024), lambda i: (i, C_AG // 1024)),
                  res(kv3), res(ika3), res(ikb3)],
        out_specs=pl.BlockSpec((tq, ATTN_WIDTH), lambda i: (i, 0)),
        out_shape=jax.ShapeDtypeStruct((tp, ATTN_WIDTH), MXU_DTYPE),
        scratch_shapes=[pltpu.VMEM((nk, tq, tk), jnp.int32),
                        pltpu.VMEM((ATTN_KV_HEADS, ATTN_REP * tq, 1), f32),
                        pltpu.VMEM((ATTN_KV_HEADS, ATTN_REP * tq, 1), f32),
                        pltpu.VMEM((ATTN_KV_HEADS, ATTN_REP * tq, LANES), f32)],
        compiler_params=pltpu.CompilerParams(
            dimension_semantics=("arbitrary",), vmem_limit_bytes=VMEM_LIMIT_BYTES),
        name="attn",
    )(q_r, iq_r, proj, proj, kv3, ika3, ikb3)


OUT_KBLK = 1024


def _outproj_kernel(mix_a_ref, mix_c_ref, w_ref, h_ref, w_norm_ref, o_ref, acc_ref):
    k = pl.program_id(1)
    n_a = (POOL_WIDTH + SSD_WIDTH) // OUT_KBLK

    @pl.when(k == 0)
    def _():
        acc_ref[...] = jnp.zeros_like(acc_ref)

    @pl.when(k < n_a)
    def _():
        acc_ref[...] += jnp.dot(mix_a_ref[...], w_ref[...], preferred_element_type=f32)

    @pl.when(k == n_a)
    def _():
        out = acc_ref[...] + jnp.dot(mix_c_ref[...], w_ref[...], preferred_element_type=f32)
        y = out * lax.rsqrt(jnp.mean(out * out, axis=-1, keepdims=True) + EPS)
        o_ref[...] = h_ref[...] + y * w_norm_ref[...]


def _outproj(mix_a, mix_c, w_out, h, post_w, tm):
    tp = h.shape[0]
    n_a = (POOL_WIDTH + SSD_WIDTH) // OUT_KBLK
    return pl.pallas_call(
        _outproj_kernel,
        grid=(tp // tm, n_a + 1),
        in_specs=[pl.BlockSpec((tm, OUT_KBLK), lambda i, k: (i, jnp.minimum(k, n_a - 1))),
                  pl.BlockSpec((tm, OUT_KBLK), lambda i, k: (i, 0)),
                  pl.BlockSpec((OUT_KBLK, D_MODEL), lambda i, k: (k, 0)),
                  pl.BlockSpec((tm, D_MODEL), lambda i, k: (i, 0)),
                  pl.BlockSpec((1, D_MODEL), lambda i, k: (0, 0))],
        out_specs=pl.BlockSpec((tm, D_MODEL), lambda i, k: (i, 0)),
        out_shape=jax.ShapeDtypeStruct((tp, D_MODEL), f32),
        scratch_shapes=[pltpu.VMEM((tm, D_MODEL), f32)],
        compiler_params=pltpu.CompilerParams(
            dimension_semantics=("arbitrary", "arbitrary"), vmem_limit_bytes=VMEM_LIMIT_BYTES),
        name="outproj",
    )(mix_a, mix_c, w_out, h, post_w)


def _pack_w_in(w_in):
    sizes = (POOL_WIDTH, POOL_WIDTH, SSD_WIDTH, SSD_WIDTH + SSD_BC, SSD_HEADS,
             ATTN_HEADS * ATTN_HEADDIM, ATTN_KV_HEADS * ATTN_HEADDIM, ATTN_KV_HEADS * ATTN_HEADDIM,
             ATTN_WIDTH, IDX_HEADS * IDX_HEADDIM, IDX_HEADDIM, IDX_HEADS)
    offs = np.cumsum((0,) + sizes)
    (pool_v, pool_g, z, xbc, dt, q, k, v, ag, iq, ik, iw) = [
        w_in[:, offs[i]:offs[i + 1]] for i in range(len(sizes))]
    pad = jnp.zeros((w_in.shape[0], N_PACK - offs[-1]), w_in.dtype)
    packed = jnp.concatenate([pool_v, pool_g, z, xbc, q, ag, iq, k, v, ik, iw, dt, pad], axis=1)
    return packed.astype(MXU_DTYPE)


def _misc_row(vec):
    return jnp.zeros((1, LANES), f32).at[0, MISC_DT:MISC_DT + SSD_HEADS].set(vec.astype(f32))


def _layer(h, tabs, topk, pre_w, post_w, w_in, pool_w, pool_scale, conv_w, conv_b, dt_bias, a_log,
           d_skip, ssd_norm_w, w_out):
    t = _tiles(h.shape[0])
    proj = _inproj(h, pre_w[None, :], _pack_w_in(w_in), t["tm"], t["tn"])
    q_r, iq_r, kv_r, ik_a, ik_b = _rope(proj, tabs, t["tm"])
    mix_a = _mixer(proj, pool_w.astype(MXU_DTYPE), pool_scale[None, :], conv_w, conv_b[None, :],
                   _misc_row(dt_bias), _misc_row(a_log),
                   jnp.repeat(d_skip.astype(f32), SSD_HEADDIM)[None, :], ssd_norm_w[None, :])
    mix_c = _attention(proj, q_r, iq_r, kv_r, ik_a, ik_b, topk, t["tq"], t["tk"])
    return _outproj(mix_a, mix_c, w_out.astype(MXU_DTYPE), h, post_w[None, :], t["tm"])


def _forward(x, meta_tokens, pre_norm_w, post_norm_w, w_in, pool_w, pool_scale, conv_w, conv_b,
             dt_bias, a_log, d_skip, ssd_norm_w, w_out):
    b, s, d = x.shape
    assert b == 1 and d == D_MODEL
    topk = min(INDEX_TOPK, s // 4)
    h = jnp.concatenate([jnp.zeros((FRONT_PAD, d), x.dtype), meta_tokens.astype(x.dtype), x[0]], axis=0)
    tabs = _rope_tables(h.shape[0])
    for l in range(pre_norm_w.shape[0]):
        h = _layer(h, tabs, topk, pre_norm_w[l], post_norm_w[l], w_in[l], pool_w[l], pool_scale[l],
                   conv_w[l], conv_b[l], dt_bias[l], a_log[l], d_skip[l], ssd_norm_w[l], w_out[l])
    return h[ROW0:][None]


def kernel(x, meta_tokens, pre_norm_w, post_norm_w, w_in, pool_w, pool_scale, conv_w, conv_b, dt_bias,
           a_log, d_skip, ssd_norm_w, w_out):
    return _forward(x, meta_tokens, pre_norm_w, post_norm_w, w_in, pool_w, pool_scale, conv_w, conv_b,
                    dt_bias, a_log, d_skip, ssd_norm_w, w_out)
```

```python
import functools
import math

import jax
import jax.numpy as jnp
import numpy as np
from jax import lax
from jax.experimental import pallas as pl
from jax.experimental.pallas import tpu as pltpu

f32 = jnp.float32
bf16 = jnp.bfloat16
MXU_DTYPE = bf16

D_MODEL = 2048
N_META = 16
EPS = 1e-6
POOL_WIDTH = 1024
POOL_WINDOWS = (2, 4, 8, 16)
POOL_GROUP = 256
SSD_WIDTH = 2048
SSD_HEADDIM = 64
SSD_HEADS = 32
SSD_GROUPS = 4
SSD_STATE = 128
SSD_CONV = 4
SSD_CHUNK = 128
SSD_BC = 2 * SSD_GROUPS * SSD_STATE
ATTN_WIDTH = 1024
ATTN_HEADDIM = 128
ATTN_HEADS = 8
ATTN_KV_HEADS = 2
ATTN_REP = ATTN_HEADS // ATTN_KV_HEADS
IDX_HEADS = 16
IDX_HEADDIM = 64
INDEX_TOPK = 256
ROPE_THETA = 500000.0
ROPE_FRACTION = 4
D_MIX = 4096

LANES = 128
VMEM_LIMIT_BYTES = 56 * 1024 * 1024

FRONT_PAD = (SSD_CHUNK - N_META % SSD_CHUNK) % SSD_CHUNK
ROW0 = FRONT_PAD + N_META

C_POOL_V = 0
C_POOL_G = 1024
C_Z = 2048
C_XS = 4096
C_BC = 6144
C_Q = 7168
C_AG = 8192
C_IQ = 9216
C_KV = 10240
C_MISC = 10752
N_USED = 10880
N_PACK = 11264
MISC_IW = IDX_HEADDIM
MISC_DT = IDX_HEADDIM + IDX_HEADS

INT_MIN = -(2 ** 31)
NEG_BIG = -1e30


ROW_ALIGN = 768


def _row_tile(tp, cap):
    return max(t for t in range(LANES, cap + 1, LANES) if tp % t == 0)


def _tiles(tp):
    assert tp % ROW_ALIGN == 0, tp
    return dict(tm_in=_row_tile(tp, 1408), tn=1024, tm=ROW_ALIGN, tq=ATTN_HEADDIM, tk=ROW_ALIGN)


def _dot(a, b):
    return jnp.dot(a.astype(MXU_DTYPE), b.astype(MXU_DTYPE), preferred_element_type=f32)


def _dot_nt(a, b):
    return lax.dot_general(a.astype(MXU_DTYPE), b.astype(MXU_DTYPE),
                           (((1,), (1,)), ((), ())), preferred_element_type=f32)


def _silu(x):
    return x * (1.0 / (1.0 + jnp.exp(-x)))


def _inproj_kernel(h_ref, w_norm_ref, w_ref, o_ref, u_ref):
    @pl.when(pl.program_id(1) == 0)
    def _():
        x = h_ref[...]
        y = x * lax.rsqrt(jnp.mean(x * x, axis=-1, keepdims=True) + EPS)
        u_ref[...] = (y * w_norm_ref[...]).astype(u_ref.dtype)

    o_ref[...] = jnp.dot(u_ref[...], w_ref[...], preferred_element_type=f32)


def _inproj(h, pre_w, w_pack, tm, tn):
    tp = h.shape[0]
    return pl.pallas_call(
        _inproj_kernel,
        grid=(tp // tm, N_PACK // tn),
        in_specs=[pl.BlockSpec((tm, D_MODEL), lambda i, j: (i, 0)),
                  pl.BlockSpec((1, D_MODEL), lambda i, j: (0, 0)),
                  pl.BlockSpec((D_MODEL, tn), lambda i, j: (0, j))],
        out_specs=pl.BlockSpec((tm, tn), lambda i, j: (i, j)),
        out_shape=jax.ShapeDtypeStruct((tp, N_PACK), f32),
        scratch_shapes=[pltpu.VMEM((tm, D_MODEL), MXU_DTYPE)],
        compiler_params=pltpu.CompilerParams(
            dimension_semantics=("arbitrary", "arbitrary"), vmem_limit_bytes=VMEM_LIMIT_BYTES),
        name="inproj",
    )(h, pre_w, w_pack)


def _rope_tile(x, cos, sin, half, lane):
    up = pltpu.roll(x, LANES - half, 1)
    down = pltpu.roll(x, half, 1)
    return x * cos + jnp.where(lane, up, down) * sin


def _rope_kernel(q_ref, iq_ref, kv_ref, misc_ref, cos_a_ref, sin_a_ref, cos_i_ref, sin_i_ref,
                 qo_ref, iqo_ref, kto_ref, vo_ref, ika_ref, ikb_ref):
    rows = q_ref.shape[0]
    lane_id = lax.broadcasted_iota(jnp.int32, (rows, LANES), 1)
    half_a = ATTN_HEADDIM // ROPE_FRACTION // 2
    half_i = IDX_HEADDIM // ROPE_FRACTION // 2
    first_a = lane_id < half_a
    first_i = (lane_id % IDX_HEADDIM) < half_i
    cos_a, sin_a = cos_a_ref[...], sin_a_ref[...]
    cos_i, sin_i = cos_i_ref[...], sin_i_ref[...]
    q_scale = ATTN_HEADDIM ** -0.5 * math.log2(math.e)
    for hd in range(ATTN_HEADS):
        sl = slice(hd * LANES, (hd + 1) * LANES)
        qo_ref[:, sl] = (_rope_tile(q_ref[:, sl], cos_a, sin_a, half_a, first_a) * q_scale
                         ).astype(qo_ref.dtype)
    for hd in range(ATTN_KV_HEADS):
        sl = slice(hd * LANES, (hd + 1) * LANES)
        k_rot = _rope_tile(kv_ref[:, sl], cos_a, sin_a, half_a, first_a)
        kto_ref[0, hd] = k_rot.T.astype(kto_ref.dtype)
    v_sl = slice(ATTN_KV_HEADS * LANES, 2 * ATTN_KV_HEADS * LANES)
    vo_ref[...] = kv_ref[:, v_sl].astype(vo_ref.dtype)
    for pair in range(IDX_HEADS * IDX_HEADDIM // LANES):
        sl = slice(pair * LANES, (pair + 1) * LANES)
        iqo_ref[:, sl] = _rope_tile(iq_ref[:, sl], cos_i, sin_i, half_i, first_i).astype(iqo_ref.dtype)
    ik = _rope_tile(misc_ref[...], cos_i, sin_i, half_i, first_i)
    low = lane_id < IDX_HEADDIM
    ika_ref[...] = jnp.where(low, ik, 0.0).astype(ika_ref.dtype)
    ikb_ref[...] = jnp.where(low, 0.0, pltpu.roll(ik, IDX_HEADDIM, 1)).astype(ikb_ref.dtype)


def _rope(proj, tabs, tm):
    tp = proj.shape[0]
    wide = lambda c: pl.BlockSpec((tm, 1024), lambda i, c=c: (i, c // 1024))
    tab = pl.BlockSpec((tm, LANES), lambda i: (i, 0))
    return pl.pallas_call(
        _rope_kernel,
        grid=(tp // tm,),
        in_specs=[wide(C_Q), wide(C_IQ),
                  pl.BlockSpec((tm, 512), lambda i: (i, C_KV // 512)),
                  pl.BlockSpec((tm, LANES), lambda i: (i, C_MISC // LANES)),
                  tab, tab, tab, tab],
        out_specs=[pl.BlockSpec((tm, 1024), lambda i: (i, 0)),
                   pl.BlockSpec((tm, 1024), lambda i: (i, 0)),
                   pl.BlockSpec((1, ATTN_KV_HEADS, ATTN_HEADDIM, tm), lambda i: (i, 0, 0, 0)),
                   pl.BlockSpec((tm, ATTN_KV_HEADS * ATTN_HEADDIM), lambda i: (i, 0)),
                   tab, tab],
        out_shape=[jax.ShapeDtypeStruct((tp, 1024), MXU_DTYPE),
                   jax.ShapeDtypeStruct((tp, 1024), MXU_DTYPE),
                   jax.ShapeDtypeStruct((tp // tm, ATTN_KV_HEADS, ATTN_HEADDIM, tm), MXU_DTYPE),
                   jax.ShapeDtypeStruct((tp, ATTN_KV_HEADS * ATTN_HEADDIM), MXU_DTYPE),
                   jax.ShapeDtypeStruct((tp, LANES), MXU_DTYPE),
                   jax.ShapeDtypeStruct((tp, LANES), MXU_DTYPE)],
        compiler_params=pltpu.CompilerParams(
            dimension_semantics=("arbitrary",), vmem_limit_bytes=VMEM_LIMIT_BYTES),
        name="rope",
    )(proj, proj, proj, proj, *tabs)


def _rope_tables(tp):
    pos = jnp.arange(tp) - FRONT_PAD

    def one(head_dim):
        rot = head_dim // ROPE_FRACTION
        half = rot // 2
        inv = jnp.power(jnp.float32(ROPE_THETA), -(jnp.arange(half, dtype=jnp.float32) * 2.0 / rot))
        ang = pos.astype(jnp.float32)[:, None] * inv[None, :]
        cos, sin = jnp.cos(ang), jnp.sin(ang)
        rest = head_dim - rot
        cos_p = jnp.concatenate([cos, cos, jnp.ones((tp, rest), f32)], axis=1)
        sin_p = jnp.concatenate([-sin, sin, jnp.zeros((tp, rest), f32)], axis=1)
        reps = LANES // head_dim
        return jnp.tile(cos_p, (1, reps)), jnp.tile(sin_p, (1, reps))

    cos_a, sin_a = one(ATTN_HEADDIM)
    cos_i, sin_i = one(IDX_HEADDIM)
    return cos_a, sin_a, cos_i, sin_i


POOL_HALO = 16
CONV_HALO = 8
CONV_SLAB = 512


def _mixer_kernel(pv_ref, pg_ref, z_ref, xs_ref, bc_ref, misc_ref,
                  pool_w_ref, pool_scale_ref, conv_w_ref, conv_b_ref, dt_bias_ref, a_log_ref,
                  d_skip_ref, norm_w_ref,
                  out_ref,
                  pbuf, xbuf, xc, state, ybuf):
    c = pl.program_id(0)
    L = SSD_CHUNK

    @pl.when(c == 0)
    def _():
        pbuf[0:POOL_HALO, :] = jnp.zeros((POOL_HALO, POOL_WIDTH), f32)
        xbuf[0:CONV_HALO, :] = jnp.zeros((CONV_HALO, SSD_WIDTH + SSD_BC), f32)
        state[...] = jnp.zeros_like(state)

    row = c * L + lax.broadcasted_iota(jnp.int32, (L, 1), 0)
    real = row >= FRONT_PAD
    tpos = row - FRONT_PAD

    pbuf[POOL_HALO:POOL_HALO + L, :] = pv_ref[...]
    for g, w in enumerate(POOL_WINDOWS):
        sl = slice(g * POOL_GROUP, (g + 1) * POOL_GROUP)
        v = pbuf[POOL_HALO:POOL_HALO + L, sl]
        acc = v
        for j in range(1, w):
            acc = acc + pbuf[POOL_HALO - j:POOL_HALO - j + L, sl]
        cnt = jnp.clip(tpos + 1, 1, w).astype(f32)
        y = acc / cnt - v
        y = _dot(y, pool_w_ref[g]) * pool_scale_ref[:, sl]
        out_ref[:, sl] = (_silu(pg_ref[:, sl]) * y).astype(out_ref.dtype)
    pbuf[0:POOL_HALO, :] = pbuf[L:L + POOL_HALO, :]

    xbuf[CONV_HALO:CONV_HALO + L, 0:SSD_WIDTH] = xs_ref[...]
    xbuf[CONV_HALO:CONV_HALO + L, SSD_WIDTH:] = bc_ref[...]
    for blk in range((SSD_WIDTH + SSD_BC) // CONV_SLAB):
        sl = slice(blk * CONV_SLAB, (blk + 1) * CONV_SLAB)
        acc = conv_b_ref[:, sl] + xbuf[CONV_HALO:CONV_HALO + L, sl] * conv_w_ref[SSD_CONV - 1:SSD_CONV, sl]
        for j in range(1, SSD_CONV):
            acc = acc + (xbuf[CONV_HALO - j:CONV_HALO - j + L, sl]
                         * conv_w_ref[SSD_CONV - 1 - j:SSD_CONV - j, sl])
        xc[:, sl] = jnp.where(real, _silu(acc), 0.0)

    dt_in = misc_ref[...] + dt_bias_ref[...]
    dt = jnp.maximum(dt_in, 0.0) + jnp.log1p(jnp.exp(-jnp.abs(dt_in)))
    dt = jnp.where(real, dt, 0.0)
    da = dt * (-jnp.exp(a_log_ref[...]))
    r_id = lax.broadcasted_iota(jnp.int32, (L, L), 0)
    c_id = lax.broadcasted_iota(jnp.int32, (L, L), 1)
    tril = r_id >= c_id
    cs = jnp.dot(tril.astype(f32), da, preferred_element_type=f32,
                 precision=lax.Precision.HIGHEST)
    cs_t = cs.T
    dt_t = dt.T

    for g in range(SSD_GROUPS):
        b_sl = slice(SSD_WIDTH + g * SSD_STATE, SSD_WIDTH + (g + 1) * SSD_STATE)
        c_sl = slice(SSD_WIDTH + SSD_GROUPS * SSD_STATE + g * SSD_STATE,
                     SSD_WIDTH + SSD_GROUPS * SSD_STATE + (g + 1) * SSD_STATE)
        bm = xc[:, b_sl]
        cm = xc[:, c_sl]
        cb = _dot_nt(cm, bm)
        bm_t = bm.T
        cm_lo = cm.astype(MXU_DTYPE)
        for r in range(SSD_HEADS // SSD_GROUPS):
            hd = g * (SSD_HEADS // SSD_GROUPS) + r
            lane = MISC_DT + hd
            h_sl = slice(hd * SSD_HEADDIM, (hd + 1) * SSD_HEADDIM)
            xs = xc[:, h_sl]
            xs_lo = xs.astype(MXU_DTYPE)
            cs_col = jnp.broadcast_to(cs[:, lane:lane + 1], (L, L))
            cs_row = cs_t[lane:lane + 1, :]
            dt_row = dt_t[lane:lane + 1, :]
            decay = jnp.exp(jnp.where(tril, cs_col - cs_row, NEG_BIG))
            y = _dot(cb * decay * dt_row, xs_lo)
            prev = state[:, h_sl]
            y = y + _dot(cm_lo, prev) * jnp.exp(cs_col[:, 0:SSD_HEADDIM])
            ybuf[:, h_sl] = y + d_skip_ref[:, h_sl] * xs
            cs_last = cs_row[:, L - 1:L]
            w_s = jnp.exp(cs_last - cs_row) * dt_row
            state[:, h_sl] = prev * jnp.exp(cs_last) + _dot(bm_t * w_s, xs_lo)

    xbuf[0:CONV_HALO, :] = xbuf[L:L + CONV_HALO, :]

    gw = SSD_WIDTH // SSD_GROUPS
    for g in range(SSD_GROUPS):
        sl = slice(g * gw, (g + 1) * gw)
        yg = ybuf[:, sl] * _silu(z_ref[:, sl])
        yg = yg * lax.rsqrt(jnp.mean(yg * yg, axis=-1, keepdims=True) + EPS)
        out_ref[:, POOL_WIDTH + g * gw:POOL_WIDTH + (g + 1) * gw] = (
            yg * norm_w_ref[:, sl]).astype(out_ref.dtype)


def _mixer(proj, pool_w, pool_scale, conv_w, conv_b, dt_bias_row, a_log_row, d_skip_wide, norm_w):
    tp = proj.shape[0]
    L = SSD_CHUNK
    col = lambda width, c: pl.BlockSpec((L, width), lambda i, c=c, width=width: (i, c // width))
    full = lambda shape: pl.BlockSpec(shape, lambda i, n=len(shape): (0,) * n)
    return pl.pallas_call(
        _mixer_kernel,
        grid=(tp // L,),
        in_specs=[col(1024, C_POOL_V), col(1024, C_POOL_G), col(2048, C_Z), col(2048, C_XS),
                  col(1024, C_BC), col(LANES, C_MISC),
                  full((len(POOL_WINDOWS), POOL_GROUP, POOL_GROUP)), full((1, POOL_WIDTH)),
                  full((SSD_CONV, SSD_WIDTH + SSD_BC)), full((1, SSD_WIDTH + SSD_BC)),
                  full((1, LANES)), full((1, LANES)), full((1, SSD_WIDTH)), full((1, SSD_WIDTH))],
        out_specs=pl.BlockSpec((L, POOL_WIDTH + SSD_WIDTH), lambda i: (i, 0)),
        out_shape=jax.ShapeDtypeStruct((tp, POOL_WIDTH + SSD_WIDTH), MXU_DTYPE),
        scratch_shapes=[pltpu.VMEM((POOL_HALO + L, POOL_WIDTH), f32),
                        pltpu.VMEM((CONV_HALO + L, SSD_WIDTH + SSD_BC), f32),
                        pltpu.VMEM((L, SSD_WIDTH + SSD_BC), f32),
                        pltpu.VMEM((SSD_STATE, SSD_WIDTH), f32),
                        pltpu.VMEM((L, SSD_WIDTH), f32)],
        compiler_params=pltpu.CompilerParams(
            dimension_semantics=("arbitrary",), vmem_limit_bytes=VMEM_LIMIT_BYTES),
        name="mixer",
    )(proj, proj, proj, proj, proj, proj, pool_w, pool_scale, conv_w, conv_b, dt_bias_row, a_log_row,
      d_skip_wide, norm_w)


def _attn_kernel(q_ref, iq_ref, misc_ref, ag_ref, kt_ref, v_ref, ika_ref, ikb_ref,
                 out_ref,
                 key_ref, m_ref, l_ref, acc_ref, *, topk, n_real):
    qb = pl.program_id(0)
    tq = q_ref.shape[0]
    tk = v_ref.shape[1]
    q0 = qb * tq
    nkb = (q0 + tq - 1) // tk + 1
    idx_scale = (IDX_HEADDIM ** -0.5) * (IDX_HEADS ** -0.5)
    n_pair = IDX_HEADS // 2

    row = q0 + lax.broadcasted_iota(jnp.int32, (tq, tk), 0)
    col_in_blk = lax.broadcasted_iota(jnp.int32, (tq, tk), 1)
    w_idx = misc_ref[...] * idx_scale
    iq_stack = jnp.concatenate([iq_ref[:, pair * LANES:(pair + 1) * LANES] for pair in range(n_pair)],
                               axis=0)

    def score_block(kb, carry):
        s_even = _dot_nt(iq_stack, ika_ref[kb])
        s_odd = _dot_nt(iq_stack, ikb_ref[kb])
        acc = jnp.zeros((tq, tk), f32)
        for pair in range(n_pair):
            rows = slice(pair * tq, (pair + 1) * tq)
            w0 = w_idx[:, MISC_IW + 2 * pair:MISC_IW + 2 * pair + 1]
            w1 = w_idx[:, MISC_IW + 2 * pair + 1:MISC_IW + 2 * pair + 2]
            acc = acc + jnp.maximum(s_even[rows], 0.0) * w0
            acc = acc + jnp.maximum(s_odd[rows], 0.0) * w1
        bits = lax.bitcast_convert_type(acc, jnp.int32)
        key = jnp.where(bits < 0, bits ^ jnp.int32(0x7FFFFFFF), bits)
        col = kb * tk + col_in_blk
        valid = (col <= row) & (col >= FRONT_PAD)
        key_ref[kb] = jnp.where(valid, key, jnp.int32(INT_MIN))
        return carry

    lax.fori_loop(0, nkb, score_block, 0)

    def count_ge(cand):
        def body(kb, cnt):
            blk = key_ref[kb]
            for j in range(tk // LANES):
                cnt = cnt + (blk[:, j * LANES:(j + 1) * LANES] >= cand).astype(jnp.int32)
            return cnt
        cnt = lax.fori_loop(0, nkb, body, jnp.zeros((tq, LANES), jnp.int32))
        return jnp.sum(cnt, axis=1, keepdims=True)

    q_row = q0 + lax.broadcasted_iota(jnp.int32, (tq, 1), 0)
    settled0 = ((q_row - FRONT_PAD < topk) | (q_row >= n_real)).astype(f32)

    def bit_cond(state):
        it, _, _, open_rows = state
        return (it < 32) & (open_rows > 0.0)

    def bit_step(state):
        it, prefix, settled, _ = state
        cand = prefix + jnp.left_shift(jnp.int32(1), 31 - it)
        total = count_ge(cand)
        prefix = jnp.where(total >= topk, cand, prefix)
        settled = jnp.maximum(settled, (total == topk).astype(f32))
        return it + 1, prefix, settled, jnp.sum(1.0 - settled)

    state0 = (jnp.int32(0), jnp.full((tq, 1), INT_MIN, jnp.int32), settled0, jnp.sum(1.0 - settled0))
    prefix = lax.while_loop(bit_cond, bit_step, state0)[1]
    thr = jnp.maximum(prefix, jnp.int32(INT_MIN + 1))

    assert tq == ATTN_HEADDIM
    m_ref[...] = jnp.full_like(m_ref, NEG_BIG)
    l_ref[...] = jnp.zeros_like(l_ref)
    acc_ref[...] = jnp.zeros_like(acc_ref)
    eye = (lax.broadcasted_iota(jnp.int32, (tq, tq), 0)
           == lax.broadcasted_iota(jnp.int32, (tq, tq), 1)).astype(MXU_DTYPE)
    q_aug = [jnp.concatenate(
        [jnp.concatenate([q_ref[:, (g * ATTN_REP + r) * LANES:(g * ATTN_REP + r + 1) * LANES], eye], axis=1)
         for r in range(ATTN_REP)], axis=0) for g in range(ATTN_KV_HEADS)]

    def attend(kb, carry):
        bias = jnp.where(key_ref[kb] >= thr, 0.0, NEG_BIG).astype(MXU_DTYPE)
        for g in range(ATTN_KV_HEADS):
            k_aug = jnp.concatenate([kt_ref[kb, g], bias], axis=0)
            s = jnp.dot(q_aug[g], k_aug, preferred_element_type=f32)
            m_old = m_ref[g]
            m_new = jnp.maximum(m_old, jnp.max(s, axis=1, keepdims=True))
            alpha = jnp.exp2(m_old - m_new)
            p = jnp.exp2(s - m_new)
            l_ref[g] = alpha * l_ref[g] + jnp.sum(p, axis=1, keepdims=True)
            v_g = v_ref[kb, :, g * LANES:(g + 1) * LANES]
            acc_ref[g] = alpha * acc_ref[g] + _dot(p, v_g)
            m_ref[g] = m_new
        return carry

    lax.fori_loop(0, nkb, attend, 0)

    for g in range(ATTN_KV_HEADS):
        o = acc_ref[g] / jnp.maximum(l_ref[g], 1e-30)
        for r in range(ATTN_REP):
            sl = slice((g * ATTN_REP + r) * LANES, (g * ATTN_REP + r + 1) * LANES)
            out_ref[:, sl] = (_silu(ag_ref[:, sl]) * o[r * tq:(r + 1) * tq, :]).astype(out_ref.dtype)


def _attention(proj, q_r, iq_r, kt_r, v_r, ik_a, ik_b, topk, n_real, tq, tk):
    tp = proj.shape[0]
    nk = tp // tk
    assert kt_r.shape == (nk, ATTN_KV_HEADS, ATTN_HEADDIM, tk)
    v3 = v_r.reshape(nk, tk, v_r.shape[1])
    ika3 = ik_a.reshape(nk, tk, LANES)
    ikb3 = ik_b.reshape(nk, tk, LANES)
    res = lambda a: pl.BlockSpec(a.shape, lambda i, n=a.ndim: (0,) * n)
    return pl.pallas_call(
        functools.partial(_attn_kernel, topk=topk, n_real=n_real),
        grid=(tp // tq,),
        in_specs=[pl.BlockSpec((tq, 1024), lambda i: (i, 0)),
                  pl.BlockSpec((tq, 1024), lambda i: (i, 0)),
                  pl.BlockSpec((tq, LANES), lambda i: (i, C_MISC // LANES)),
                  pl.BlockSpec((tq, 1024), lambda i: (i, C_AG // 1024)),
                  res(kt_r), res(v3), res(ika3), res(ikb3)],
        out_specs=pl.BlockSpec((tq, ATTN_WIDTH), lambda i: (i, 0)),
        out_shape=jax.ShapeDtypeStruct((tp, ATTN_WIDTH), MXU_DTYPE),
        scratch_shapes=[pltpu.VMEM((nk, tq, tk), jnp.int32),
                        pltpu.VMEM((ATTN_KV_HEADS, ATTN_REP * tq, 1), f32),
                        pltpu.VMEM((ATTN_KV_HEADS, ATTN_REP * tq, 1), f32),
                        pltpu.VMEM((ATTN_KV_HEADS, ATTN_REP * tq, LANES), f32)],
        compiler_params=pltpu.CompilerParams(
            dimension_semantics=("arbitrary",), vmem_limit_bytes=VMEM_LIMIT_BYTES),
        name="attn",
    )(q_r, iq_r, proj, proj, kt_r, v3, ika3, ikb3)


OUT_KBLK = 1024


def _outproj_kernel(mix_a_ref, mix_c_ref, w_ref, h_ref, w_norm_ref, o_ref, acc_ref):
    k = pl.program_id(1)
    n_a = (POOL_WIDTH + SSD_WIDTH) // OUT_KBLK

    @pl.when(k == 0)
    def _():
        acc_ref[...] = jnp.zeros_like(acc_ref)

    @pl.when(k < n_a)
    def _():
        acc_ref[...] += jnp.dot(mix_a_ref[...], w_ref[...], preferred_element_type=f32)

    @pl.when(k == n_a)
    def _():
        out = acc_ref[...] + jnp.dot(mix_c_ref[...], w_ref[...], preferred_element_type=f32)
        y = out * lax.rsqrt(jnp.mean(out * out, axis=-1, keepdims=True) + EPS)
        o_ref[...] = h_ref[...] + y * w_norm_ref[...]


def _outproj(mix_a, mix_c, w_out, h, post_w, tm):
    tp = h.shape[0]
    n_a = (POOL_WIDTH + SSD_WIDTH) // OUT_KBLK
    return pl.pallas_call(
        _outproj_kernel,
        grid=(tp // tm, n_a + 1),
        in_specs=[pl.BlockSpec((tm, OUT_KBLK), lambda i, k: (i, jnp.minimum(k, n_a - 1))),
                  pl.BlockSpec((tm, OUT_KBLK), lambda i, k: (i, 0)),
                  pl.BlockSpec((OUT_KBLK, D_MODEL), lambda i, k: (k, 0)),
                  pl.BlockSpec((tm, D_MODEL), lambda i, k: (i, 0)),
                  pl.BlockSpec((1, D_MODEL), lambda i, k: (0, 0))],
        out_specs=pl.BlockSpec((tm, D_MODEL), lambda i, k: (i, 0)),
        out_shape=jax.ShapeDtypeStruct((tp, D_MODEL), f32),
        scratch_shapes=[pltpu.VMEM((tm, D_MODEL), f32)],
        compiler_params=pltpu.CompilerParams(
            dimension_semantics=("arbitrary", "arbitrary"), vmem_limit_bytes=VMEM_LIMIT_BYTES),
        name="outproj",
    )(mix_a, mix_c, w_out, h, post_w)


def _pack_w_in(w_in):
    sizes = (POOL_WIDTH, POOL_WIDTH, SSD_WIDTH, SSD_WIDTH + SSD_BC, SSD_HEADS,
             ATTN_HEADS * ATTN_HEADDIM, ATTN_KV_HEADS * ATTN_HEADDIM, ATTN_KV_HEADS * ATTN_HEADDIM,
             ATTN_WIDTH, IDX_HEADS * IDX_HEADDIM, IDX_HEADDIM, IDX_HEADS)
    offs = np.cumsum((0,) + sizes)
    (pool_v, pool_g, z, xbc, dt, q, k, v, ag, iq, ik, iw) = [
        w_in[:, offs[i]:offs[i + 1]] for i in range(len(sizes))]
    pad = jnp.zeros((w_in.shape[0], N_PACK - offs[-1]), w_in.dtype)
    packed = jnp.concatenate([pool_v, pool_g, z, xbc, q, ag, iq, k, v, ik, iw, dt, pad], axis=1)
    return packed.astype(MXU_DTYPE)


def _misc_row(vec):
    return jnp.zeros((1, LANES), f32).at[0, MISC_DT:MISC_DT + SSD_HEADS].set(vec.astype(f32))


def _layer(h, tabs, topk, n_real, pre_w, post_w, w_in, pool_w, pool_scale, conv_w, conv_b, dt_bias, a_log,
           d_skip, ssd_norm_w, w_out):
    t = _tiles(h.shape[0])
    proj = _inproj(h, pre_w[None, :], _pack_w_in(w_in), t["tm_in"], t["tn"])
    q_r, iq_r, kt_r, v_r, ik_a, ik_b = _rope(proj, tabs, t["tk"])
    mix_a = _mixer(proj, pool_w.astype(MXU_DTYPE), pool_scale[None, :], conv_w, conv_b[None, :],
                   _misc_row(dt_bias), _misc_row(a_log),
                   jnp.repeat(d_skip.astype(f32), SSD_HEADDIM)[None, :], ssd_norm_w[None, :])
    mix_c = _attention(proj, q_r, iq_r, kt_r, v_r, ik_a, ik_b, topk, n_real, t["tq"], t["tk"])
    return _outproj(mix_a, mix_c, w_out.astype(MXU_DTYPE), h, post_w[None, :], t["tm"])


def _forward(x, meta_tokens, pre_norm_w, post_norm_w, w_in, pool_w, pool_scale, conv_w, conv_b,
             dt_bias, a_log, d_skip, ssd_norm_w, w_out):
    b, s, d = x.shape
    assert b == 1 and d == D_MODEL
    topk = min(INDEX_TOPK, s // 4)
    n_real = ROW0 + s
    tp = -(-n_real // ROW_ALIGN) * ROW_ALIGN
    h = jnp.concatenate([jnp.zeros((FRONT_PAD, d), x.dtype), meta_tokens.astype(x.dtype), x[0],
                         jnp.zeros((tp - n_real, d), x.dtype)], axis=0)
    tabs = _rope_tables(tp)
    for l in range(pre_norm_w.shape[0]):
        h = _layer(h, tabs, topk, n_real, pre_norm_w[l], post_norm_w[l], w_in[l], pool_w[l], pool_scale[l],
                   conv_w[l], conv_b[l], dt_bias[l], a_log[l], d_skip[l], ssd_norm_w[l], w_out[l])
    return h[ROW0:n_real][None]


def kernel(x, meta_tokens, pre_norm_w, post_norm_w, w_in, pool_w, pool_scale, conv_w, conv_b, dt_bias,
           a_log, d_skip, ssd_norm_w, w_out):
    return _forward(x, meta_tokens, pre_norm_w, post_norm_w, w_in, pool_w, pool_scale, conv_w, conv_b,
                    dt_bias, a_log, d_skip, ssd_norm_w, w_out)
```

```python
import functools
import math

import jax
import jax.numpy as jnp
import numpy as np
from jax import lax
from jax.experimental import pallas as pl
from jax.experimental.pallas import tpu as pltpu

f32 = jnp.float32
bf16 = jnp.bfloat16
MXU_DTYPE = bf16

D_MODEL = 2048
N_META = 16
EPS = 1e-6
POOL_WIDTH = 1024
POOL_WINDOWS = (2, 4, 8, 16)
POOL_GROUP = 256
SSD_WIDTH = 2048
SSD_HEADDIM = 64
SSD_HEADS = 32
SSD_GROUPS = 4
SSD_STATE = 128
SSD_CONV = 4
SSD_CHUNK = 128
SSD_BC = 2 * SSD_GROUPS * SSD_STATE
ATTN_WIDTH = 1024
ATTN_HEADDIM = 128
ATTN_HEADS = 8
ATTN_KV_HEADS = 2
ATTN_REP = ATTN_HEADS // ATTN_KV_HEADS
IDX_HEADS = 16
IDX_HEADDIM = 64
INDEX_TOPK = 256
ROPE_THETA = 500000.0
ROPE_FRACTION = 4
D_MIX = 4096

LANES = 128
VMEM_LIMIT_BYTES = 56 * 1024 * 1024

FRONT_PAD = (SSD_CHUNK - N_META % SSD_CHUNK) % SSD_CHUNK
ROW0 = FRONT_PAD + N_META

C_POOL_V = 0
C_POOL_G = 1024
C_Z = 2048
C_XS = 4096
C_BC = 6144
C_Q = 7168
C_AG = 8192
C_IQ = 9216
C_KV = 10240
C_MISC = 10752
N_USED = 10880
N_PACK = 11264
MISC_IW = IDX_HEADDIM
MISC_DT = IDX_HEADDIM + IDX_HEADS

INT_MIN = -(2 ** 31)
NEG_BIG = -1e30


ROW_ALIGN = 768


def _row_tile(tp, cap):
    return max(t for t in range(LANES, cap + 1, LANES) if tp % t == 0)


def _tiles(tp):
    assert tp % ROW_ALIGN == 0, tp
    return dict(tm_in=_row_tile(tp, 1408), tn=1024, tm=ROW_ALIGN, tq=ATTN_HEADDIM, tk=ROW_ALIGN)


def _dot(a, b):
    return jnp.dot(a.astype(MXU_DTYPE), b.astype(MXU_DTYPE), preferred_element_type=f32)


def _dot_nt(a, b):
    return lax.dot_general(a.astype(MXU_DTYPE), b.astype(MXU_DTYPE),
                           (((1,), (1,)), ((), ())), preferred_element_type=f32)


def _silu(x):
    return x * (1.0 / (1.0 + jnp.exp(-x)))


def _inproj_kernel(h_ref, w_norm_ref, w_ref, o_ref, u_ref):
    @pl.when(pl.program_id(1) == 0)
    def _():
        x = h_ref[...]
        y = x * lax.rsqrt(jnp.mean(x * x, axis=-1, keepdims=True) + EPS)
        u_ref[...] = (y * w_norm_ref[...]).astype(u_ref.dtype)

    o_ref[...] = jnp.dot(u_ref[...], w_ref[...], preferred_element_type=f32)


def _inproj(h, pre_w, w_pack, tm, tn):
    tp = h.shape[0]
    return pl.pallas_call(
        _inproj_kernel,
        grid=(tp // tm, N_PACK // tn),
        in_specs=[pl.BlockSpec((tm, D_MODEL), lambda i, j: (i, 0)),
                  pl.BlockSpec((1, D_MODEL), lambda i, j: (0, 0)),
                  pl.BlockSpec((D_MODEL, tn), lambda i, j: (0, j))],
        out_specs=pl.BlockSpec((tm, tn), lambda i, j: (i, j)),
        out_shape=jax.ShapeDtypeStruct((tp, N_PACK), f32),
        scratch_shapes=[pltpu.VMEM((tm, D_MODEL), MXU_DTYPE)],
        compiler_params=pltpu.CompilerParams(
            dimension_semantics=("arbitrary", "arbitrary"), vmem_limit_bytes=VMEM_LIMIT_BYTES),
        name="inproj",
    )(h, pre_w, w_pack)


def _rope_tile(x, cos, sin, half, lane):
    up = pltpu.roll(x, LANES - half, 1)
    down = pltpu.roll(x, half, 1)
    return x * cos + jnp.where(lane, up, down) * sin


def _rope_kernel(q_ref, iq_ref, kv_ref, misc_ref, cos_a_ref, sin_a_ref, cos_i_ref, sin_i_ref,
                 qo_ref, iqo_ref, kto_ref, vo_ref, ika_ref, ikb_ref):
    rows = q_ref.shape[0]
    lane_id = lax.broadcasted_iota(jnp.int32, (rows, LANES), 1)
    half_a = ATTN_HEADDIM // ROPE_FRACTION // 2
    half_i = IDX_HEADDIM // ROPE_FRACTION // 2
    first_a = lane_id < half_a
    first_i = (lane_id % IDX_HEADDIM) < half_i
    cos_a, sin_a = cos_a_ref[...], sin_a_ref[...]
    cos_i, sin_i = cos_i_ref[...], sin_i_ref[...]
    q_scale = ATTN_HEADDIM ** -0.5 * math.log2(math.e)
    for hd in range(ATTN_HEADS):
        sl = slice(hd * LANES, (hd + 1) * LANES)
        qo_ref[:, sl] = (_rope_tile(q_ref[:, sl], cos_a, sin_a, half_a, first_a) * q_scale
                         ).astype(qo_ref.dtype)
    for hd in range(ATTN_KV_HEADS):
        sl = slice(hd * LANES, (hd + 1) * LANES)
        k_rot = _rope_tile(kv_ref[:, sl], cos_a, sin_a, half_a, first_a)
        kto_ref[0, hd] = k_rot.T.astype(kto_ref.dtype)
    v_sl = slice(ATTN_KV_HEADS * LANES, 2 * ATTN_KV_HEADS * LANES)
    vo_ref[...] = kv_ref[:, v_sl].astype(vo_ref.dtype)
    for pair in range(IDX_HEADS * IDX_HEADDIM // LANES):
        sl = slice(pair * LANES, (pair + 1) * LANES)
        iqo_ref[:, sl] = _rope_tile(iq_ref[:, sl], cos_i, sin_i, half_i, first_i).astype(iqo_ref.dtype)
    ik = _rope_tile(misc_ref[...], cos_i, sin_i, half_i, first_i)
    low = lane_id < IDX_HEADDIM
    ika_ref[...] = jnp.where(low, ik, 0.0).astype(ika_ref.dtype)
    ikb_ref[...] = jnp.where(low, 0.0, pltpu.roll(ik, IDX_HEADDIM, 1)).astype(ikb_ref.dtype)


def _rope(proj, tabs, tm):
    tp = proj.shape[0]
    wide = lambda c: pl.BlockSpec((tm, 1024), lambda i, c=c: (i, c // 1024))
    tab = pl.BlockSpec((tm, LANES), lambda i: (i, 0))
    return pl.pallas_call(
        _rope_kernel,
        grid=(tp // tm,),
        in_specs=[wide(C_Q), wide(C_IQ),
                  pl.BlockSpec((tm, 512), lambda i: (i, C_KV // 512)),
                  pl.BlockSpec((tm, LANES), lambda i: (i, C_MISC // LANES)),
                  tab, tab, tab, tab],
        out_specs=[pl.BlockSpec((tm, 1024), lambda i: (i, 0)),
                   pl.BlockSpec((tm, 1024), lambda i: (i, 0)),
                   pl.BlockSpec((1, ATTN_KV_HEADS, ATTN_HEADDIM, tm), lambda i: (i, 0, 0, 0)),
                   pl.BlockSpec((tm, ATTN_KV_HEADS * ATTN_HEADDIM), lambda i: (i, 0)),
                   tab, tab],
        out_shape=[jax.ShapeDtypeStruct((tp, 1024), MXU_DTYPE),
                   jax.ShapeDtypeStruct((tp, 1024), MXU_DTYPE),
                   jax.ShapeDtypeStruct((tp // tm, ATTN_KV_HEADS, ATTN_HEADDIM, tm), MXU_DTYPE),
                   jax.ShapeDtypeStruct((tp, ATTN_KV_HEADS * ATTN_HEADDIM), MXU_DTYPE),
                   jax.ShapeDtypeStruct((tp, LANES), MXU_DTYPE),
                   jax.ShapeDtypeStruct((tp, LANES), MXU_DTYPE)],
        compiler_params=pltpu.CompilerParams(
            dimension_semantics=("arbitrary",), vmem_limit_bytes=VMEM_LIMIT_BYTES),
        name="rope",
    )(proj, proj, proj, proj, *tabs)


def _rope_tables(tp):
    pos = jnp.arange(tp) - FRONT_PAD

    def one(head_dim):
        rot = head_dim // ROPE_FRACTION
        half = rot // 2
        inv = jnp.power(jnp.float32(ROPE_THETA), -(jnp.arange(half, dtype=jnp.float32) * 2.0 / rot))
        ang = pos.astype(jnp.float32)[:, None] * inv[None, :]
        cos, sin = jnp.cos(ang), jnp.sin(ang)
        rest = head_dim - rot
        cos_p = jnp.concatenate([cos, cos, jnp.ones((tp, rest), f32)], axis=1)
        sin_p = jnp.concatenate([-sin, sin, jnp.zeros((tp, rest), f32)], axis=1)
        reps = LANES // head_dim
        return jnp.tile(cos_p, (1, reps)), jnp.tile(sin_p, (1, reps))

    cos_a, sin_a = one(ATTN_HEADDIM)
    cos_i, sin_i = one(IDX_HEADDIM)
    return cos_a, sin_a, cos_i, sin_i


POOL_HALO = 16
CONV_HALO = 8
CONV_SLAB = 512


def _mixer_kernel(pv_ref, pg_ref, z_ref, xs_ref, bc_ref, misc_ref,
                  pool_w_ref, pool_scale_ref, conv_w_ref, conv_b_ref, dt_bias_ref, a_log_ref,
                  d_skip_ref, norm_w_ref,
                  out_ref,
                  pbuf, xbuf, xc, state, ybuf):
    c = pl.program_id(0)
    L = SSD_CHUNK

    @pl.when(c == 0)
    def _():
        pbuf[0:POOL_HALO, :] = jnp.zeros((POOL_HALO, POOL_WIDTH), f32)
        xbuf[0:CONV_HALO, :] = jnp.zeros((CONV_HALO, SSD_WIDTH + SSD_BC), f32)
        state[...] = jnp.zeros_like(state)

    row = c * L + lax.broadcasted_iota(jnp.int32, (L, 1), 0)
    real = row >= FRONT_PAD
    tpos = row - FRONT_PAD

    pbuf[POOL_HALO:POOL_HALO + L, :] = pv_ref[...]
    for g, w in enumerate(POOL_WINDOWS):
        sl = slice(g * POOL_GROUP, (g + 1) * POOL_GROUP)
        v = pbuf[POOL_HALO:POOL_HALO + L, sl]
        acc = v
        for j in range(1, w):
            acc = acc + pbuf[POOL_HALO - j:POOL_HALO - j + L, sl]
        cnt = jnp.clip(tpos + 1, 1, w).astype(f32)
        y = acc / cnt - v
        y = _dot(y, pool_w_ref[g]) * pool_scale_ref[:, sl]
        out_ref[:, sl] = (_silu(pg_ref[:, sl]) * y).astype(out_ref.dtype)
    pbuf[0:POOL_HALO, :] = pbuf[L:L + POOL_HALO, :]

    xbuf[CONV_HALO:CONV_HALO + L, 0:SSD_WIDTH] = xs_ref[...]
    xbuf[CONV_HALO:CONV_HALO + L, SSD_WIDTH:] = bc_ref[...]
    for blk in range((SSD_WIDTH + SSD_BC) // CONV_SLAB):
        sl = slice(blk * CONV_SLAB, (blk + 1) * CONV_SLAB)
        acc = conv_b_ref[:, sl] + xbuf[CONV_HALO:CONV_HALO + L, sl] * conv_w_ref[SSD_CONV - 1:SSD_CONV, sl]
        for j in range(1, SSD_CONV):
            acc = acc + (xbuf[CONV_HALO - j:CONV_HALO - j + L, sl]
                         * conv_w_ref[SSD_CONV - 1 - j:SSD_CONV - j, sl])
        xc[:, sl] = jnp.where(real, _silu(acc), 0.0)

    dt_in = misc_ref[...] + dt_bias_ref[...]
    dt = jnp.maximum(dt_in, 0.0) + jnp.log1p(jnp.exp(-jnp.abs(dt_in)))
    dt = jnp.where(real, dt, 0.0)
    da = dt * (-jnp.exp(a_log_ref[...]))
    r_id = lax.broadcasted_iota(jnp.int32, (L, L), 0)
    c_id = lax.broadcasted_iota(jnp.int32, (L, L), 1)
    tril = r_id >= c_id
    cs = jnp.dot(tril.astype(f32), da, preferred_element_type=f32,
                 precision=lax.Precision.HIGHEST)
    cs_t = cs.T
    dt_t = dt.T

    for g in range(SSD_GROUPS):
        b_sl = slice(SSD_WIDTH + g * SSD_STATE, SSD_WIDTH + (g + 1) * SSD_STATE)
        c_sl = slice(SSD_WIDTH + SSD_GROUPS * SSD_STATE + g * SSD_STATE,
                     SSD_WIDTH + SSD_GROUPS * SSD_STATE + (g + 1) * SSD_STATE)
        bm = xc[:, b_sl]
        cm = xc[:, c_sl]
        cb = _dot_nt(cm, bm)
        bm_t = bm.T
        cm_lo = cm.astype(MXU_DTYPE)
        for r in range(SSD_HEADS // SSD_GROUPS):
            hd = g * (SSD_HEADS // SSD_GROUPS) + r
            lane = MISC_DT + hd
            h_sl = slice(hd * SSD_HEADDIM, (hd + 1) * SSD_HEADDIM)
            xs = xc[:, h_sl]
            xs_lo = xs.astype(MXU_DTYPE)
            cs_col = jnp.broadcast_to(cs[:, lane:lane + 1], (L, L))
            cs_row = cs_t[lane:lane + 1, :]
            dt_row = dt_t[lane:lane + 1, :]
            decay = jnp.exp(jnp.where(tril, cs_col - cs_row, NEG_BIG))
            y = _dot(cb * decay * dt_row, xs_lo)
            prev = state[:, h_sl]
            y = y + _dot(cm_lo, prev) * jnp.exp(cs_col[:, 0:SSD_HEADDIM])
            ybuf[:, h_sl] = y + d_skip_ref[:, h_sl] * xs
            cs_last = cs_row[:, L - 1:L]
            w_s = jnp.exp(cs_last - cs_row) * dt_row
            state[:, h_sl] = prev * jnp.exp(cs_last) + _dot(bm_t * w_s, xs_lo)

    xbuf[0:CONV_HALO, :] = xbuf[L:L + CONV_HALO, :]

    gw = SSD_WIDTH // SSD_GROUPS
    for g in range(SSD_GROUPS):
        sl = slice(g * gw, (g + 1) * gw)
        yg = ybuf[:, sl] * _silu(z_ref[:, sl])
        yg = yg * lax.rsqrt(jnp.mean(yg * yg, axis=-1, keepdims=True) + EPS)
        out_ref[:, POOL_WIDTH + g * gw:POOL_WIDTH + (g + 1) * gw] = (
            yg * norm_w_ref[:, sl]).astype(out_ref.dtype)


def _mixer(proj, pool_w, pool_scale, conv_w, conv_b, dt_bias_row, a_log_row, d_skip_wide, norm_w):
    tp = proj.shape[0]
    L = SSD_CHUNK
    col = lambda width, c: pl.BlockSpec((L, width), lambda i, c=c, width=width: (i, c // width))
    full = lambda shape: pl.BlockSpec(shape, lambda i, n=len(shape): (0,) * n)
    return pl.pallas_call(
        _mixer_kernel,
        grid=(tp // L,),
        in_specs=[col(1024, C_POOL_V), col(1024, C_POOL_G), col(2048, C_Z), col(2048, C_XS),
                  col(1024, C_BC), col(LANES, C_MISC),
                  full((len(POOL_WINDOWS), POOL_GROUP, POOL_GROUP)), full((1, POOL_WIDTH)),
                  full((SSD_CONV, SSD_WIDTH + SSD_BC)), full((1, SSD_WIDTH + SSD_BC)),
                  full((1, LANES)), full((1, LANES)), full((1, SSD_WIDTH)), full((1, SSD_WIDTH))],
        out_specs=pl.BlockSpec((L, POOL_WIDTH + SSD_WIDTH), lambda i: (i, 0)),
        out_shape=jax.ShapeDtypeStruct((tp, POOL_WIDTH + SSD_WIDTH), MXU_DTYPE),
        scratch_shapes=[pltpu.VMEM((POOL_HALO + L, POOL_WIDTH), f32),
                        pltpu.VMEM((CONV_HALO + L, SSD_WIDTH + SSD_BC), f32),
                        pltpu.VMEM((L, SSD_WIDTH + SSD_BC), f32),
                        pltpu.VMEM((SSD_STATE, SSD_WIDTH), f32),
                        pltpu.VMEM((L, SSD_WIDTH), f32)],
        compiler_params=pltpu.CompilerParams(
            dimension_semantics=("arbitrary",), vmem_limit_bytes=VMEM_LIMIT_BYTES),
        name="mixer",
    )(proj, proj, proj, proj, proj, proj, pool_w, pool_scale, conv_w, conv_b, dt_bias_row, a_log_row,
      d_skip_wide, norm_w)


def _attn_kernel(q_ref, iq_ref, misc_ref, ag_ref, kt_ref, v_ref, ika_ref, ikb_ref,
                 out_ref,
                 key_ref, top_ref, m_ref, l_ref, acc_ref, s0_ref, s1_ref, *, topk, n_real):
    qb = pl.program_id(0)
    tq = q_ref.shape[0]
    tk = v_ref.shape[1]
    q0 = qb * tq
    nkb = (q0 + tq - 1) // tk + 1
    idx_scale = (IDX_HEADDIM ** -0.5) * (IDX_HEADS ** -0.5)
    n_pair = IDX_HEADS // 2

    row = q0 + lax.broadcasted_iota(jnp.int32, (tq, tk), 0)
    col_in_blk = lax.broadcasted_iota(jnp.int32, (tq, tk), 1)
    w_idx = misc_ref[...] * idx_scale
    iq_stack = jnp.concatenate([iq_ref[:, pair * LANES:(pair + 1) * LANES] for pair in range(n_pair)],
                               axis=0)

    def score_block(kb, carry):
        s_even = _dot_nt(iq_stack, ika_ref[kb])
        s_odd = _dot_nt(iq_stack, ikb_ref[kb])
        acc = jnp.zeros((tq, tk), f32)
        for pair in range(n_pair):
            rows = slice(pair * tq, (pair + 1) * tq)
            w0 = w_idx[:, MISC_IW + 2 * pair:MISC_IW + 2 * pair + 1]
            w1 = w_idx[:, MISC_IW + 2 * pair + 1:MISC_IW + 2 * pair + 2]
            acc = acc + jnp.maximum(s_even[rows], 0.0) * w0
            acc = acc + jnp.maximum(s_odd[rows], 0.0) * w1
        bits = lax.bitcast_convert_type(acc, jnp.int32)
        key = jnp.where(bits < 0, bits ^ jnp.int32(0x7FFFFFFF), bits)
        col = kb * tk + col_in_blk
        valid = (col <= row) & (col >= FRONT_PAD)
        key_ref[kb] = jnp.where(valid, key, jnp.int32(INT_MIN))
        top = lax.bitcast_convert_type(bits & jnp.int32(-65536), f32)
        top_ref[kb] = jnp.where(valid, top, -jnp.inf).astype(top_ref.dtype)
        return carry

    lax.fori_loop(0, nkb, score_block, 0)

    def count_top_ge(cand):
        one = jnp.ones((tq, LANES), top_ref.dtype)
        zero = jnp.zeros((tq, LANES), top_ref.dtype)

        def body(kb, cnt):
            blk = top_ref[kb]
            for j in range(tk // LANES):
                cnt = cnt + jnp.where(blk[:, j * LANES:(j + 1) * LANES] >= cand, one, zero)
            return cnt
        cnt = lax.fori_loop(0, nkb, body, zero)
        return jnp.sum(cnt.astype(f32), axis=1, keepdims=True).astype(jnp.int32)

    def count_ge(cand):
        def body(kb, cnt):
            blk = key_ref[kb]
            for j in range(tk // LANES):
                cnt = cnt + (blk[:, j * LANES:(j + 1) * LANES] >= cand).astype(jnp.int32)
            return cnt
        cnt = lax.fori_loop(0, nkb, body, jnp.zeros((tq, LANES), jnp.int32))
        return jnp.sum(cnt, axis=1, keepdims=True)

    q_row = q0 + lax.broadcasted_iota(jnp.int32, (tq, 1), 0)
    settled0 = ((q_row - FRONT_PAD < topk) | (q_row >= n_real)).astype(f32)

    def bit_cond(n_bits):
        def cond(state):
            it, _, _, open_rows = state
            return (it < n_bits) & (open_rows > 0.0)
        return cond

    def top_step(state):
        it, prefix, settled, _ = state
        cand = prefix + jnp.left_shift(jnp.int32(1), 15 - it)
        pattern = jnp.where(cand < 0, cand ^ jnp.int32(0x7FFF), cand)
        cand_f = lax.bitcast_convert_type(jnp.left_shift(pattern, 16), f32)
        total = count_top_ge(jnp.broadcast_to(cand_f, (tq, LANES)).astype(top_ref.dtype))
        prefix = jnp.where(total >= topk, cand, prefix)
        settled = jnp.maximum(settled, (total == topk).astype(f32))
        return it + 1, prefix, settled, jnp.sum(1.0 - settled)

    def low_step(state):
        it, prefix, settled, _ = state
        cand = prefix + jnp.left_shift(jnp.int32(1), 31 - it)
        total = count_ge(cand)
        prefix = jnp.where(total >= topk, cand, prefix)
        settled = jnp.maximum(settled, (total == topk).astype(f32))
        return it + 1, prefix, settled, jnp.sum(1.0 - settled)

    state = (jnp.int32(0), jnp.full((tq, 1), -(2 ** 15), jnp.int32), settled0, jnp.sum(1.0 - settled0))
    _, prefix16, settled, open_rows = lax.while_loop(bit_cond(16), top_step, state)
    state = (jnp.int32(16), jnp.left_shift(prefix16, 16), settled, open_rows)
    prefix = lax.while_loop(bit_cond(32), low_step, state)[1]
    thr = jnp.maximum(prefix, jnp.int32(INT_MIN + 1))

    assert tq == ATTN_HEADDIM
    m_ref[...] = jnp.full_like(m_ref, NEG_BIG)
    l_ref[...] = jnp.zeros_like(l_ref)
    acc_ref[...] = jnp.zeros_like(acc_ref)
    eye = (lax.broadcasted_iota(jnp.int32, (tq, tq), 0)
           == lax.broadcasted_iota(jnp.int32, (tq, tq), 1)).astype(MXU_DTYPE)
    q_aug = [jnp.concatenate(
        [jnp.concatenate([q_ref[:, (g * ATTN_REP + r) * LANES:(g * ATTN_REP + r + 1) * LANES], eye], axis=1)
         for r in range(ATTN_REP)], axis=0) for g in range(ATTN_KV_HEADS)]

    def logits(kb, s_ref):
        bias = jnp.where(key_ref[kb] >= thr, 0.0, NEG_BIG).astype(MXU_DTYPE)
        for g in range(ATTN_KV_HEADS):
            k_aug = jnp.concatenate([kt_ref[kb, g], bias], axis=0)
            s_ref[g] = jnp.dot(q_aug[g], k_aug, preferred_element_type=f32)

    def softmax_pv(kb, s_ref):
        for g in range(ATTN_KV_HEADS):
            s = s_ref[g]
            m_old = m_ref[g]
            m_new = jnp.maximum(m_old, jnp.max(s, axis=1, keepdims=True))
            alpha = jnp.exp2(m_old - m_new)
            p = jnp.exp2(s - m_new)
            l_ref[g] = alpha * l_ref[g] + jnp.sum(p, axis=1, keepdims=True)
            v_g = v_ref[kb, :, g * LANES:(g + 1) * LANES]
            acc_ref[g] = alpha * acc_ref[g] + _dot(p, v_g)
            m_ref[g] = m_new

    last = kt_ref.shape[0] - 1
    logits(0, s0_ref)

    def attend_pair(j, carry):
        logits(2 * j + 1, s1_ref)
        softmax_pv(2 * j, s0_ref)
        logits(jnp.minimum(2 * j + 2, last), s0_ref)
        softmax_pv(2 * j + 1, s1_ref)
        return carry

    lax.fori_loop(0, nkb // 2, attend_pair, 0)

    @pl.when(nkb % 2 == 1)
    def _():
        softmax_pv(nkb - 1, s0_ref)

    for g in range(ATTN_KV_HEADS):
        o = acc_ref[g] / jnp.maximum(l_ref[g], 1e-30)
        for r in range(ATTN_REP):
            sl = slice((g * ATTN_REP + r) * LANES, (g * ATTN_REP + r + 1) * LANES)
            out_ref[:, sl] = (_silu(ag_ref[:, sl]) * o[r * tq:(r + 1) * tq, :]).astype(out_ref.dtype)


def _attention(proj, q_r, iq_r, kt_r, v_r, ik_a, ik_b, topk, n_real, tq, tk):
    tp = proj.shape[0]
    nk = tp // tk
    assert kt_r.shape == (nk, ATTN_KV_HEADS, ATTN_HEADDIM, tk)
    v3 = v_r.reshape(nk, tk, v_r.shape[1])
    ika3 = ik_a.reshape(nk, tk, LANES)
    ikb3 = ik_b.reshape(nk, tk, LANES)
    res = lambda a: pl.BlockSpec(a.shape, lambda i, n=a.ndim: (0,) * n)
    return pl.pallas_call(
        functools.partial(_attn_kernel, topk=topk, n_real=n_real),
        grid=(tp // tq,),
        in_specs=[pl.BlockSpec((tq, 1024), lambda i: (i, 0)),
                  pl.BlockSpec((tq, 1024), lambda i: (i, 0)),
                  pl.BlockSpec((tq, LANES), lambda i: (i, C_MISC // LANES)),
                  pl.BlockSpec((tq, 1024), lambda i: (i, C_AG // 1024)),
                  res(kt_r), res(v3), res(ika3), res(ikb3)],
        out_specs=pl.BlockSpec((tq, ATTN_WIDTH), lambda i: (i, 0)),
        out_shape=jax.ShapeDtypeStruct((tp, ATTN_WIDTH), MXU_DTYPE),
        scratch_shapes=[pltpu.VMEM((nk, tq, tk), jnp.int32),
                        pltpu.VMEM((nk, tq, tk), bf16),
                        pltpu.VMEM((ATTN_KV_HEADS, ATTN_REP * tq, 1), f32),
                        pltpu.VMEM((ATTN_KV_HEADS, ATTN_REP * tq, 1), f32),
                        pltpu.VMEM((ATTN_KV_HEADS, ATTN_REP * tq, LANES), f32),
                        pltpu.VMEM((ATTN_KV_HEADS, ATTN_REP * tq, tk), f32),
                        pltpu.VMEM((ATTN_KV_HEADS, ATTN_REP * tq, tk), f32)],
        compiler_params=pltpu.CompilerParams(
            dimension_semantics=("arbitrary",), vmem_limit_bytes=VMEM_LIMIT_BYTES),
        name="attn",
    )(q_r, iq_r, proj, proj, kt_r, v3, ika3, ikb3)


OUT_KBLK = 1024


def _outproj_kernel(mix_a_ref, mix_c_ref, w_ref, h_ref, w_norm_ref, o_ref, acc_ref):
    k = pl.program_id(1)
    n_a = (POOL_WIDTH + SSD_WIDTH) // OUT_KBLK

    @pl.when(k == 0)
    def _():
        acc_ref[...] = jnp.zeros_like(acc_ref)

    @pl.when(k < n_a)
    def _():
        acc_ref[...] += jnp.dot(mix_a_ref[...], w_ref[...], preferred_element_type=f32)

    @pl.when(k == n_a)
    def _():
        out = acc_ref[...] + jnp.dot(mix_c_ref[...], w_ref[...], preferred_element_type=f32)
        y = out * lax.rsqrt(jnp.mean(out * out, axis=-1, keepdims=True) + EPS)
        o_ref[...] = h_ref[...] + y * w_norm_ref[...]


def _outproj(mix_a, mix_c, w_out, h, post_w, tm):
    tp = h.shape[0]
    n_a = (POOL_WIDTH + SSD_WIDTH) // OUT_KBLK
    return pl.pallas_call(
        _outproj_kernel,
        grid=(tp // tm, n_a + 1),
        in_specs=[pl.BlockSpec((tm, OUT_KBLK), lambda i, k: (i, jnp.minimum(k, n_a - 1))),
                  pl.BlockSpec((tm, OUT_KBLK), lambda i, k: (i, 0)),
                  pl.BlockSpec((OUT_KBLK, D_MODEL), lambda i, k: (k, 0)),
                  pl.BlockSpec((tm, D_MODEL), lambda i, k: (i, 0)),
                  pl.BlockSpec((1, D_MODEL), lambda i, k: (0, 0))],
        out_specs=pl.BlockSpec((tm, D_MODEL), lambda i, k: (i, 0)),
        out_shape=jax.ShapeDtypeStruct((tp, D_MODEL), f32),
        scratch_shapes=[pltpu.VMEM((tm, D_MODEL), f32)],
        compiler_params=pltpu.CompilerParams(
            dimension_semantics=("arbitrary", "arbitrary"), vmem_limit_bytes=VMEM_LIMIT_BYTES),
        name="outproj",
    )(mix_a, mix_c, w_out, h, post_w)


def _pack_w_in(w_in):
    sizes = (POOL_WIDTH, POOL_WIDTH, SSD_WIDTH, SSD_WIDTH + SSD_BC, SSD_HEADS,
             ATTN_HEADS * ATTN_HEADDIM, ATTN_KV_HEADS * ATTN_HEADDIM, ATTN_KV_HEADS * ATTN_HEADDIM,
             ATTN_WIDTH, IDX_HEADS * IDX_HEADDIM, IDX_HEADDIM, IDX_HEADS)
    offs = np.cumsum((0,) + sizes)
    (pool_v, pool_g, z, xbc, dt, q, k, v, ag, iq, ik, iw) = [
        w_in[:, offs[i]:offs[i + 1]] for i in range(len(sizes))]
    pad = jnp.zeros((w_in.shape[0], N_PACK - offs[-1]), w_in.dtype)
    packed = jnp.concatenate([pool_v, pool_g, z, xbc, q, ag, iq, k, v, ik, iw, dt, pad], axis=1)
    return packed.astype(MXU_DTYPE)


def _misc_row(vec):
    return jnp.zeros((1, LANES), f32).at[0, MISC_DT:MISC_DT + SSD_HEADS].set(vec.astype(f32))


def _layer(h, tabs, topk, n_real, pre_w, post_w, w_in, pool_w, pool_scale, conv_w, conv_b, dt_bias, a_log,
           d_skip, ssd_norm_w, w_out):
    t = _tiles(h.shape[0])
    proj = _inproj(h, pre_w[None, :], _pack_w_in(w_in), t["tm_in"], t["tn"])
    q_r, iq_r, kt_r, v_r, ik_a, ik_b = _rope(proj, tabs, t["tk"])
    mix_a = _mixer(proj, pool_w.astype(MXU_DTYPE), pool_scale[None, :], conv_w, conv_b[None, :],
                   _misc_row(dt_bias), _misc_row(a_log),
                   jnp.repeat(d_skip.astype(f32), SSD_HEADDIM)[None, :], ssd_norm_w[None, :])
    mix_c = _attention(proj, q_r, iq_r, kt_r, v_r, ik_a, ik_b, topk, n_real, t["tq"], t["tk"])
    return _outproj(mix_a, mix_c, w_out.astype(MXU_DTYPE), h, post_w[None, :], t["tm"])


def _forward(x, meta_tokens, pre_norm_w, post_norm_w, w_in, pool_w, pool_scale, conv_w, conv_b,
             dt_bias, a_log, d_skip, ssd_norm_w, w_out):
    b, s, d = x.shape
    assert b == 1 and d == D_MODEL
    topk = min(INDEX_TOPK, s // 4)
    n_real = ROW0 + s
    tp = -(-n_real // ROW_ALIGN) * ROW_ALIGN
    h = jnp.concatenate([jnp.zeros((FRONT_PAD, d), x.dtype), meta_tokens.astype(x.dtype), x[0],
                         jnp.zeros((tp - n_real, d), x.dtype)], axis=0)
    tabs = _rope_tables(tp)
    for l in range(pre_norm_w.shape[0]):
        h = _layer(h, tabs, topk, n_real, pre_norm_w[l], post_norm_w[l], w_in[l], pool_w[l], pool_scale[l],
                   conv_w[l], conv_b[l], dt_bias[l], a_log[l], d_skip[l], ssd_norm_w[l], w_out[l])
    return h[ROW0:n_real][None]


def kernel(x, meta_tokens, pre_norm_w, post_norm_w, w_in, pool_w, pool_scale, conv_w, conv_b, dt_bias,
           a_log, d_skip, ssd_norm_w, w_out):
    return _forward(x, meta_tokens, pre_norm_w, post_norm_w, w_in, pool_w, pool_scale, conv_w, conv_b,
                    dt_bias, a_log, d_skip, ssd_norm_w, w_out)
```

```python
import functools
import math

import jax
import jax.numpy as jnp
import numpy as np
from jax import lax
from jax.experimental import pallas as pl
from jax.experimental.pallas import tpu as pltpu

f32 = jnp.float32
bf16 = jnp.bfloat16
MXU_DTYPE = bf16

D_MODEL = 2048
N_META = 16
EPS = 1e-6
POOL_WIDTH = 1024
POOL_WINDOWS = (2, 4, 8, 16)
POOL_GROUP = 256
SSD_WIDTH = 2048
SSD_HEADDIM = 64
SSD_HEADS = 32
SSD_GROUPS = 4
SSD_STATE = 128
SSD_CONV = 4
SSD_CHUNK = 128
SSD_BC = 2 * SSD_GROUPS * SSD_STATE
ATTN_WIDTH = 1024
ATTN_HEADDIM = 128
ATTN_HEADS = 8
ATTN_KV_HEADS = 2
ATTN_REP = ATTN_HEADS // ATTN_KV_HEADS
IDX_HEADS = 16
IDX_HEADDIM = 64
INDEX_TOPK = 256
ROPE_THETA = 500000.0
ROPE_FRACTION = 4
D_MIX = 4096

LANES = 128
SUBLANES = 8
PACK_ROWS = 16
VMEM_LIMIT_BYTES = 56 * 1024 * 1024

FRONT_PAD = (SSD_CHUNK - N_META % SSD_CHUNK) % SSD_CHUNK
ROW0 = FRONT_PAD + N_META

C_POOL_V = 0
C_POOL_G = 1024
C_Z = 2048
C_XS = 4096
C_BC = 6144
C_Q = 7168
C_AG = 8192
C_IQ = 9216
C_KV = 10240
C_MISC = 10752
N_USED = 10880
N_PACK = 11264
MISC_IW = IDX_HEADDIM
MISC_DT = IDX_HEADDIM + IDX_HEADS

COUNT_CHAINS = 8
LOW_BITS_PER_TRIP = 4
INT_MIN = -(2 ** 31)
NEG_BIG = -1e30


ROW_ALIGN = 768


def _row_tile(tp, cap):
    return max(t for t in range(LANES, cap + 1, LANES) if tp % t == 0)


def _tiles(tp):
    assert tp % ROW_ALIGN == 0, tp
    return dict(tm_in=_row_tile(tp, 1408), tn=1024, tm=ROW_ALIGN, tq=ATTN_HEADDIM, tk=ROW_ALIGN)


def _dot(a, b):
    return jnp.dot(a.astype(MXU_DTYPE), b.astype(MXU_DTYPE), preferred_element_type=f32)


def _dot_nt(a, b):
    return lax.dot_general(a.astype(MXU_DTYPE), b.astype(MXU_DTYPE),
                           (((1,), (1,)), ((), ())), preferred_element_type=f32)


def _silu(x):
    return x * (1.0 / (1.0 + jnp.exp(-x)))


def _inproj_kernel(h_ref, w_norm_ref, w_ref, o_ref, u_ref):
    @pl.when(pl.program_id(1) == 0)
    def _():
        x = h_ref[...]
        y = x * lax.rsqrt(jnp.mean(x * x, axis=-1, keepdims=True) + EPS)
        u_ref[...] = (y * w_norm_ref[...]).astype(u_ref.dtype)

    o_ref[...] = jnp.dot(u_ref[...], w_ref[...], preferred_element_type=f32)


def _inproj(h, pre_w, w_pack, tm, tn):
    tp = h.shape[0]
    return pl.pallas_call(
        _inproj_kernel,
        grid=(tp // tm, N_PACK // tn),
        in_specs=[pl.BlockSpec((tm, D_MODEL), lambda i, j: (i, 0)),
                  pl.BlockSpec((1, D_MODEL), lambda i, j: (0, 0)),
                  pl.BlockSpec((D_MODEL, tn), lambda i, j: (0, j))],
        out_specs=pl.BlockSpec((tm, tn), lambda i, j: (i, j)),
        out_shape=jax.ShapeDtypeStruct((tp, N_PACK), f32),
        scratch_shapes=[pltpu.VMEM((tm, D_MODEL), MXU_DTYPE)],
        compiler_params=pltpu.CompilerParams(
            dimension_semantics=("arbitrary", "arbitrary"), vmem_limit_bytes=VMEM_LIMIT_BYTES),
        name="inproj",
    )(h, pre_w, w_pack)


def _rope_tile(x, cos, sin, half, lane):
    up = pltpu.roll(x, LANES - half, 1)
    down = pltpu.roll(x, half, 1)
    return x * cos + jnp.where(lane, up, down) * sin


def _rope_kernel(q_ref, iq_ref, kv_ref, misc_ref, cos_a_ref, sin_a_ref, cos_i_ref, sin_i_ref,
                 qo_ref, iqo_ref, kvo_ref, ika_ref, ikb_ref):
    rows = q_ref.shape[0]
    lane_id = lax.broadcasted_iota(jnp.int32, (rows, LANES), 1)
    half_a = ATTN_HEADDIM // ROPE_FRACTION // 2
    half_i = IDX_HEADDIM // ROPE_FRACTION // 2
    first_a = lane_id < half_a
    first_i = (lane_id % IDX_HEADDIM) < half_i
    cos_a, sin_a = cos_a_ref[...], sin_a_ref[...]
    cos_i, sin_i = cos_i_ref[...], sin_i_ref[...]
    q_scale = ATTN_HEADDIM ** -0.5 * math.log2(math.e)
    for hd in range(ATTN_HEADS):
        sl = slice(hd * LANES, (hd + 1) * LANES)
        qo_ref[:, sl] = (_rope_tile(q_ref[:, sl], cos_a, sin_a, half_a, first_a) * q_scale
                         ).astype(qo_ref.dtype)
    for hd in range(ATTN_KV_HEADS):
        sl = slice(hd * LANES, (hd + 1) * LANES)
        kvo_ref[:, sl] = _rope_tile(kv_ref[:, sl], cos_a, sin_a, half_a, first_a).astype(kvo_ref.dtype)
    v_sl = slice(ATTN_KV_HEADS * LANES, 2 * ATTN_KV_HEADS * LANES)
    kvo_ref[:, v_sl] = kv_ref[:, v_sl].astype(kvo_ref.dtype)
    for pair in range(IDX_HEADS * IDX_HEADDIM // LANES):
        sl = slice(pair * LANES, (pair + 1) * LANES)
        iqo_ref[:, sl] = _rope_tile(iq_ref[:, sl], cos_i, sin_i, half_i, first_i).astype(iqo_ref.dtype)
    ik = _rope_tile(misc_ref[...], cos_i, sin_i, half_i, first_i)
    low = lane_id < IDX_HEADDIM
    ika_ref[...] = jnp.where(low, ik, 0.0).astype(ika_ref.dtype)
    ikb_ref[...] = jnp.where(low, 0.0, pltpu.roll(ik, IDX_HEADDIM, 1)).astype(ikb_ref.dtype)


def _rope(proj, tabs, tm):
    tp = proj.shape[0]
    wide = lambda c: pl.BlockSpec((tm, 1024), lambda i, c=c: (i, c // 1024))
    tab = pl.BlockSpec((tm, LANES), lambda i: (i, 0))
    return pl.pallas_call(
        _rope_kernel,
        grid=(tp // tm,),
        in_specs=[wide(C_Q), wide(C_IQ),
                  pl.BlockSpec((tm, 512), lambda i: (i, C_KV // 512)),
                  pl.BlockSpec((tm, LANES), lambda i: (i, C_MISC // LANES)),
                  tab, tab, tab, tab],
        out_specs=[pl.BlockSpec((tm, 1024), lambda i: (i, 0)),
                   pl.BlockSpec((tm, 1024), lambda i: (i, 0)),
                   pl.BlockSpec((tm, 512), lambda i: (i, 0)),
                   tab, tab],
        out_shape=[jax.ShapeDtypeStruct((tp, 1024), MXU_DTYPE),
                   jax.ShapeDtypeStruct((tp, 1024), MXU_DTYPE),
                   jax.ShapeDtypeStruct((tp, 512), MXU_DTYPE),
                   jax.ShapeDtypeStruct((tp, LANES), MXU_DTYPE),
                   jax.ShapeDtypeStruct((tp, LANES), MXU_DTYPE)],
        compiler_params=pltpu.CompilerParams(
            dimension_semantics=("arbitrary",), vmem_limit_bytes=VMEM_LIMIT_BYTES),
        name="rope",
    )(proj, proj, proj, proj, *tabs)


def _rope_tables(tp):
    pos = jnp.arange(tp) - FRONT_PAD

    def one(head_dim):
        rot = head_dim // ROPE_FRACTION
        half = rot // 2
        inv = jnp.power(jnp.float32(ROPE_THETA), -(jnp.arange(half, dtype=jnp.float32) * 2.0 / rot))
        ang = pos.astype(jnp.float32)[:, None] * inv[None, :]
        cos, sin = jnp.cos(ang), jnp.sin(ang)
        rest = head_dim - rot
        cos_p = jnp.concatenate([cos, cos, jnp.ones((tp, rest), f32)], axis=1)
        sin_p = jnp.concatenate([-sin, sin, jnp.zeros((tp, rest), f32)], axis=1)
        reps = LANES // head_dim
        return jnp.tile(cos_p, (1, reps)), jnp.tile(sin_p, (1, reps))

    cos_a, sin_a = one(ATTN_HEADDIM)
    cos_i, sin_i = one(IDX_HEADDIM)
    return cos_a, sin_a, cos_i, sin_i


POOL_HALO = 16
CONV_HALO = 8
CONV_SLAB = 512


def _mixer_kernel(pv_ref, pg_ref, z_ref, xs_ref, bc_ref, misc_ref,
                  pool_w_ref, pool_scale_ref, conv_w_ref, conv_b_ref, dt_bias_ref, a_log_ref,
                  d_skip_ref, norm_w_ref,
                  out_ref,
                  pbuf, xbuf, xc, state, ybuf):
    c = pl.program_id(0)
    L = SSD_CHUNK

    @pl.when(c == 0)
    def _():
        pbuf[0:POOL_HALO, :] = jnp.zeros((POOL_HALO, POOL_WIDTH), f32)
        xbuf[0:CONV_HALO, :] = jnp.zeros((CONV_HALO, SSD_WIDTH + SSD_BC), f32)
        state[...] = jnp.zeros_like(state)

    row = c * L + lax.broadcasted_iota(jnp.int32, (L, 1), 0)
    real = row >= FRONT_PAD
    tpos = row - FRONT_PAD

    pbuf[POOL_HALO:POOL_HALO + L, :] = pv_ref[...]
    for g, w in enumerate(POOL_WINDOWS):
        sl = slice(g * POOL_GROUP, (g + 1) * POOL_GROUP)
        v = pbuf[POOL_HALO:POOL_HALO + L, sl]
        acc = v
        for j in range(1, w):
            acc = acc + pbuf[POOL_HALO - j:POOL_HALO - j + L, sl]
        cnt = jnp.clip(tpos + 1, 1, w).astype(f32)
        y = acc / cnt - v
        y = _dot(y, pool_w_ref[g]) * pool_scale_ref[:, sl]
        out_ref[:, sl] = (_silu(pg_ref[:, sl]) * y).astype(out_ref.dtype)
    pbuf[0:POOL_HALO, :] = pbuf[L:L + POOL_HALO, :]

    xbuf[CONV_HALO:CONV_HALO + L, 0:SSD_WIDTH] = xs_ref[...]
    xbuf[CONV_HALO:CONV_HALO + L, SSD_WIDTH:] = bc_ref[...]
    for blk in range((SSD_WIDTH + SSD_BC) // CONV_SLAB):
        sl = slice(blk * CONV_SLAB, (blk + 1) * CONV_SLAB)
        acc = conv_b_ref[:, sl] + xbuf[CONV_HALO:CONV_HALO + L, sl] * conv_w_ref[SSD_CONV - 1:SSD_CONV, sl]
        for j in range(1, SSD_CONV):
            acc = acc + (xbuf[CONV_HALO - j:CONV_HALO - j + L, sl]
                         * conv_w_ref[SSD_CONV - 1 - j:SSD_CONV - j, sl])
        xc[:, sl] = jnp.where(real, _silu(acc), 0.0)

    dt_in = misc_ref[...] + dt_bias_ref[...]
    dt = jnp.maximum(dt_in, 0.0) + jnp.log1p(jnp.exp(-jnp.abs(dt_in)))
    dt = jnp.where(real, dt, 0.0)
    da = dt * (-jnp.exp(a_log_ref[...]))
    r_id = lax.broadcasted_iota(jnp.int32, (L, L), 0)
    c_id = lax.broadcasted_iota(jnp.int32, (L, L), 1)
    tril = r_id >= c_id
    cs = jnp.dot(tril.astype(f32), da, preferred_element_type=f32,
                 precision=lax.Precision.HIGHEST)
    cs_t = cs.T
    dt_t = dt.T

    for g in range(SSD_GROUPS):
        b_sl = slice(SSD_WIDTH + g * SSD_STATE, SSD_WIDTH + (g + 1) * SSD_STATE)
        c_sl = slice(SSD_WIDTH + SSD_GROUPS * SSD_STATE + g * SSD_STATE,
                     SSD_WIDTH + SSD_GROUPS * SSD_STATE + (g + 1) * SSD_STATE)
        bm = xc[:, b_sl]
        cm = xc[:, c_sl]
        cb = _dot_nt(cm, bm)
        bm_t = bm.T
        cm_lo = cm.astype(MXU_DTYPE)
        for r in range(SSD_HEADS // SSD_GROUPS):
            hd = g * (SSD_HEADS // SSD_GROUPS) + r
            lane = MISC_DT + hd
            h_sl = slice(hd * SSD_HEADDIM, (hd + 1) * SSD_HEADDIM)
            xs = xc[:, h_sl]
            xs_lo = xs.astype(MXU_DTYPE)
            cs_col = jnp.broadcast_to(cs[:, lane:lane + 1], (L, L))
            cs_row = cs_t[lane:lane + 1, :]
            dt_row = dt_t[lane:lane + 1, :]
            decay = jnp.exp(jnp.where(tril, cs_col - cs_row, NEG_BIG))
            y = _dot(cb * decay * dt_row, xs_lo)
            prev = state[:, h_sl]
            y = y + _dot(cm_lo, prev) * jnp.exp(cs_col[:, 0:SSD_HEADDIM])
            ybuf[:, h_sl] = y + d_skip_ref[:, h_sl] * xs
            cs_last = cs_row[:, L - 1:L]
            w_s = jnp.exp(cs_last - cs_row) * dt_row
            state[:, h_sl] = prev * jnp.exp(cs_last) + _dot(bm_t * w_s, xs_lo)

    xbuf[0:CONV_HALO, :] = xbuf[L:L + CONV_HALO, :]

    gw = SSD_WIDTH // SSD_GROUPS
    for g in range(SSD_GROUPS):
        sl = slice(g * gw, (g + 1) * gw)
        yg = ybuf[:, sl] * _silu(z_ref[:, sl])
        yg = yg * lax.rsqrt(jnp.mean(yg * yg, axis=-1, keepdims=True) + EPS)
        out_ref[:, POOL_WIDTH + g * gw:POOL_WIDTH + (g + 1) * gw] = (
            yg * norm_w_ref[:, sl]).astype(out_ref.dtype)


def _mixer(proj, pool_w, pool_scale, conv_w, conv_b, dt_bias_row, a_log_row, d_skip_wide, norm_w):
    tp = proj.shape[0]
    L = SSD_CHUNK
    col = lambda width, c: pl.BlockSpec((L, width), lambda i, c=c, width=width: (i, c // width))
    full = lambda shape: pl.BlockSpec(shape, lambda i, n=len(shape): (0,) * n)
    return pl.pallas_call(
        _mixer_kernel,
        grid=(tp // L,),
        in_specs=[col(1024, C_POOL_V), col(1024, C_POOL_G), col(2048, C_Z), col(2048, C_XS),
                  col(1024, C_BC), col(LANES, C_MISC),
                  full((len(POOL_WINDOWS), POOL_GROUP, POOL_GROUP)), full((1, POOL_WIDTH)),
                  full((SSD_CONV, SSD_WIDTH + SSD_BC)), full((1, SSD_WIDTH + SSD_BC)),
                  full((1, LANES)), full((1, LANES)), full((1, SSD_WIDTH)), full((1, SSD_WIDTH))],
        out_specs=pl.BlockSpec((L, POOL_WIDTH + SSD_WIDTH), lambda i: (i, 0)),
        out_shape=jax.ShapeDtypeStruct((tp, POOL_WIDTH + SSD_WIDTH), MXU_DTYPE),
        scratch_shapes=[pltpu.VMEM((POOL_HALO + L, POOL_WIDTH), f32),
                        pltpu.VMEM((CONV_HALO + L, SSD_WIDTH + SSD_BC), f32),
                        pltpu.VMEM((L, SSD_WIDTH + SSD_BC), f32),
                        pltpu.VMEM((SSD_STATE, SSD_WIDTH), f32),
                        pltpu.VMEM((L, SSD_WIDTH), f32)],
        compiler_params=pltpu.CompilerParams(
            dimension_semantics=("arbitrary",), vmem_limit_bytes=VMEM_LIMIT_BYTES),
        name="mixer",
    )(proj, proj, proj, proj, proj, proj, pool_w, pool_scale, conv_w, conv_b, dt_bias_row, a_log_row,
      d_skip_wide, norm_w)


def _attn_kernel(q_ref, iq_ref, misc_ref, ag_ref, kv_ref, ika_ref, ikb_ref,
                 out_ref,
                 key_ref, top_ref, m_ref, l_ref, acc_ref, s0_ref, s1_ref, *, topk, n_real):
    qb = pl.program_id(0)
    tq = q_ref.shape[0]
    tk = kv_ref.shape[1]
    q0 = qb * tq
    nkb = (q0 + tq - 1) // tk + 1
    idx_scale = (IDX_HEADDIM ** -0.5) * (IDX_HEADS ** -0.5)
    n_pair = IDX_HEADS // 2

    q_idx = q0 + lax.broadcasted_iota(jnp.int32, (tk, tq), 1)
    k_in_blk = lax.broadcasted_iota(jnp.int32, (tk, tq), 0)
    w_t = (misc_ref[...] * idx_scale).T
    iq_two = [jnp.concatenate([iq_ref[:, (2 * pp) * LANES:(2 * pp + 1) * LANES],
                               iq_ref[:, (2 * pp + 1) * LANES:(2 * pp + 2) * LANES]], axis=0)
              for pp in range(n_pair // 2)]

    def score_block(kb, carry):
        ka = ika_ref[kb]
        kb_ = ikb_ref[kb]
        acc = jnp.zeros((tk, tq), f32)
        for pp in range(n_pair // 2):
            s_even = _dot_nt(ka, iq_two[pp])
            s_odd = _dot_nt(kb_, iq_two[pp])
            for half in range(2):
                pair = 2 * pp + half
                cols = slice(half * tq, (half + 1) * tq)
                w0 = w_t[MISC_IW + 2 * pair:MISC_IW + 2 * pair + 1, :]
                w1 = w_t[MISC_IW + 2 * pair + 1:MISC_IW + 2 * pair + 2, :]
                acc = acc + jnp.maximum(s_even[:, cols], 0.0) * w0
                acc = acc + jnp.maximum(s_odd[:, cols], 0.0) * w1
        bits = lax.bitcast_convert_type(acc, jnp.int32)
        key = jnp.where(bits < 0, bits ^ jnp.int32(0x7FFFFFFF), bits)
        k_idx = kb * tk + k_in_blk
        valid = (k_idx <= q_idx) & (k_idx >= FRONT_PAD)
        key_ref[kb] = jnp.where(valid, key, jnp.int32(INT_MIN))
        top = lax.bitcast_convert_type(bits & jnp.int32(-65536), f32)
        top_ref[kb] = jnp.where(valid, top, -jnp.inf).astype(top_ref.dtype)
        return carry

    lax.fori_loop(0, nkb, score_block, 0)

    def count_top_ge(cand):
        cand = jnp.broadcast_to(cand, (PACK_ROWS, tq)).astype(top_ref.dtype)
        one = jnp.ones((PACK_ROWS, tq), top_ref.dtype)
        zero = jnp.zeros((PACK_ROWS, tq), top_ref.dtype)

        def body(kb, parts):
            blk = top_ref[kb]
            parts = list(parts)
            for j in range(tk // PACK_ROWS):
                hit = jnp.where(blk[j * PACK_ROWS:(j + 1) * PACK_ROWS, :] >= cand, one, zero)
                parts[j % COUNT_CHAINS] = parts[j % COUNT_CHAINS] + hit
            return tuple(parts)
        parts = lax.fori_loop(0, nkb, body, (zero,) * COUNT_CHAINS)
        cnt = sum(p.astype(f32) for p in parts)
        return jnp.sum(cnt, axis=0, keepdims=True).astype(jnp.int32)

    def count_ge(cand):
        cand = jnp.broadcast_to(cand, (SUBLANES, tq))

        def body(kb, parts):
            blk = key_ref[kb]
            parts = list(parts)
            for j in range(tk // SUBLANES):
                hit = (blk[j * SUBLANES:(j + 1) * SUBLANES, :] >= cand).astype(jnp.int32)
                parts[j % COUNT_CHAINS] = parts[j % COUNT_CHAINS] + hit
            return tuple(parts)
        parts = lax.fori_loop(0, nkb, body, (jnp.zeros((SUBLANES, tq), jnp.int32),) * COUNT_CHAINS)
        return jnp.sum(sum(parts), axis=0, keepdims=True)

    q_row = q0 + lax.broadcasted_iota(jnp.int32, (1, tq), 1)
    settled0 = ((q_row - FRONT_PAD < topk) | (q_row >= n_real)).astype(f32)

    def top_step(it, state):
        prefix, settled = state
        cand = prefix + jnp.left_shift(jnp.int32(1), 15 - it)
        pattern = jnp.where(cand < 0, cand ^ jnp.int32(0x7FFF), cand)
        total = count_top_ge(lax.bitcast_convert_type(jnp.left_shift(pattern, 16), f32))
        prefix = jnp.where(total >= topk, cand, prefix)
        return prefix, jnp.maximum(settled, (total == topk).astype(f32))

    def low_cond(state):
        it, _, _, open_rows = state
        return (it < 32) & (open_rows > 0.0)

    def low_steps(state):
        it, prefix, settled, _ = state
        for b in range(LOW_BITS_PER_TRIP):
            cand = prefix + jnp.left_shift(jnp.int32(1), 31 - b - it)
            total = count_ge(cand)
            prefix = jnp.where(total >= topk, cand, prefix)
            settled = jnp.maximum(settled, (total == topk).astype(f32))
        return it + LOW_BITS_PER_TRIP, prefix, settled, jnp.sum(1.0 - settled)

    prefix16, settled = lax.fori_loop(0, 16, top_step, (jnp.full((1, tq), -(2 ** 15), jnp.int32), settled0))
    state = (jnp.int32(16), jnp.left_shift(prefix16, 16), settled, jnp.sum(1.0 - settled))
    prefix = lax.while_loop(low_cond, low_steps, state)[1]
    thr = jnp.maximum(prefix, jnp.int32(INT_MIN + 1))

    assert tq == ATTN_HEADDIM
    m_ref[...] = jnp.full_like(m_ref, NEG_BIG)
    l_ref[...] = jnp.zeros_like(l_ref)
    acc_ref[...] = jnp.zeros_like(acc_ref)
    eye = (lax.broadcasted_iota(jnp.int32, (tq, tq), 0)
           == lax.broadcasted_iota(jnp.int32, (tq, tq), 1)).astype(MXU_DTYPE)
    q_aug = [jnp.concatenate(
        [jnp.concatenate([q_ref[:, (g * ATTN_REP + r) * LANES:(g * ATTN_REP + r + 1) * LANES], eye], axis=1)
         for r in range(ATTN_REP)], axis=0) for g in range(ATTN_KV_HEADS)]

    def logits(kb, s_ref):
        bias = jnp.where(key_ref[kb] >= thr, 0.0, NEG_BIG).astype(MXU_DTYPE)
        for g in range(ATTN_KV_HEADS):
            k_aug = jnp.concatenate([kv_ref[kb, :, g * LANES:(g + 1) * LANES], bias], axis=1)
            s_ref[g] = _dot_nt(q_aug[g], k_aug)

    def softmax_pv(kb, s_ref):
        for g in range(ATTN_KV_HEADS):
            s = s_ref[g]
            m_old = m_ref[g]
            m_new = jnp.maximum(m_old, jnp.max(s, axis=1, keepdims=True))
            alpha = jnp.exp2(m_old - m_new)
            p = jnp.exp2(s - m_new)
            l_ref[g] = alpha * l_ref[g] + jnp.sum(p, axis=1, keepdims=True)
            v_g = kv_ref[kb, :, (ATTN_KV_HEADS + g) * LANES:(ATTN_KV_HEADS + g + 1) * LANES]
            acc_ref[g] = alpha * acc_ref[g] + _dot(p, v_g)
            m_ref[g] = m_new

    last = kv_ref.shape[0] - 1
    logits(0, s0_ref)

    def attend_pair(j, carry):
        logits(2 * j + 1, s1_ref)
        softmax_pv(2 * j, s0_ref)
        logits(jnp.minimum(2 * j + 2, last), s0_ref)
        softmax_pv(2 * j + 1, s1_ref)
        return carry

    lax.fori_loop(0, nkb // 2, attend_pair, 0)

    @pl.when(nkb % 2 == 1)
    def _():
        softmax_pv(nkb - 1, s0_ref)

    for g in range(ATTN_KV_HEADS):
        o = acc_ref[g] / jnp.maximum(l_ref[g], 1e-30)
        for r in range(ATTN_REP):
            sl = slice((g * ATTN_REP + r) * LANES, (g * ATTN_REP + r + 1) * LANES)
            out_ref[:, sl] = (_silu(ag_ref[:, sl]) * o[r * tq:(r + 1) * tq, :]).astype(out_ref.dtype)


def _attention(proj, q_r, iq_r, kv_r, ik_a, ik_b, topk, n_real, tq, tk):
    tp = proj.shape[0]
    nk = tp // tk
    assert nk * (tk // PACK_ROWS // COUNT_CHAINS) <= 256
    kv3 = kv_r.reshape(nk, tk, kv_r.shape[1])
    ika3 = ik_a.reshape(nk, tk, LANES)
    ikb3 = ik_b.reshape(nk, tk, LANES)
    res = lambda a: pl.BlockSpec(a.shape, lambda i, n=a.ndim: (0,) * n)
    return pl.pallas_call(
        functools.partial(_attn_kernel, topk=topk, n_real=n_real),
        grid=(tp // tq,),
        in_specs=[pl.BlockSpec((tq, 1024), lambda i: (i, 0)),
                  pl.BlockSpec((tq, 1024), lambda i: (i, 0)),
                  pl.BlockSpec((tq, LANES), lambda i: (i, C_MISC // LANES)),
                  pl.BlockSpec((tq, 1024), lambda i: (i, C_AG // 1024)),
                  res(kv3), res(ika3), res(ikb3)],
        out_specs=pl.BlockSpec((tq, ATTN_WIDTH), lambda i: (i, 0)),
        out_shape=jax.ShapeDtypeStruct((tp, ATTN_WIDTH), MXU_DTYPE),
        scratch_shapes=[pltpu.VMEM((nk, tk, tq), jnp.int32),
                        pltpu.VMEM((nk, tk, tq), bf16),
                        pltpu.VMEM((ATTN_KV_HEADS, ATTN_REP * tq, 1), f32),
                        pltpu.VMEM((ATTN_KV_HEADS, ATTN_REP * tq, 1), f32),
                        pltpu.VMEM((ATTN_KV_HEADS, ATTN_REP * tq, LANES), f32),
                        pltpu.VMEM((ATTN_KV_HEADS, ATTN_REP * tq, tk), f32),
                        pltpu.VMEM((ATTN_KV_HEADS, ATTN_REP * tq, tk), f32)],
        compiler_params=pltpu.CompilerParams(
            dimension_semantics=("arbitrary",), vmem_limit_bytes=VMEM_LIMIT_BYTES),
        name="attn",
    )(q_r, iq_r, proj, proj, kv3, ika3, ikb3)


OUT_KBLK = 1024


def _outproj_kernel(mix_a_ref, mix_c_ref, w_ref, h_ref, w_norm_ref, o_ref, acc_ref):
    k = pl.program_id(1)
    n_a = (POOL_WIDTH + SSD_WIDTH) // OUT_KBLK

    @pl.when(k == 0)
    def _():
        acc_ref[...] = jnp.zeros_like(acc_ref)

    @pl.when(k < n_a)
    def _():
        acc_ref[...] += jnp.dot(mix_a_ref[...], w_ref[...], preferred_element_type=f32)

    @pl.when(k == n_a)
    def _():
        out = acc_ref[...] + jnp.dot(mix_c_ref[...], w_ref[...], preferred_element_type=f32)
        y = out * lax.rsqrt(jnp.mean(out * out, axis=-1, keepdims=True) + EPS)
        o_ref[...] = h_ref[...] + y * w_norm_ref[...]


def _outproj(mix_a, mix_c, w_out, h, post_w, tm):
    tp = h.shape[0]
    n_a = (POOL_WIDTH + SSD_WIDTH) // OUT_KBLK
    return pl.pallas_call(
        _outproj_kernel,
        grid=(tp // tm, n_a + 1),
        in_specs=[pl.BlockSpec((tm, OUT_KBLK), lambda i, k: (i, jnp.minimum(k, n_a - 1))),
                  pl.BlockSpec((tm, OUT_KBLK), lambda i, k: (i, 0)),
                  pl.BlockSpec((OUT_KBLK, D_MODEL), lambda i, k: (k, 0)),
                  pl.BlockSpec((tm, D_MODEL), lambda i, k: (i, 0)),
                  pl.BlockSpec((1, D_MODEL), lambda i, k: (0, 0))],
        out_specs=pl.BlockSpec((tm, D_MODEL), lambda i, k: (i, 0)),
        out_shape=jax.ShapeDtypeStruct((tp, D_MODEL), f32),
        scratch_shapes=[pltpu.VMEM((tm, D_MODEL), f32)],
        compiler_params=pltpu.CompilerParams(
            dimension_semantics=("arbitrary", "arbitrary"), vmem_limit_bytes=VMEM_LIMIT_BYTES),
        name="outproj",
    )(mix_a, mix_c, w_out, h, post_w)


_SRC_SIZES = (POOL_WIDTH, POOL_WIDTH, SSD_WIDTH, SSD_WIDTH + SSD_BC, SSD_HEADS,
              ATTN_HEADS * ATTN_HEADDIM, ATTN_KV_HEADS * ATTN_HEADDIM, ATTN_KV_HEADS * ATTN_HEADDIM,
              ATTN_WIDTH, IDX_HEADS * IDX_HEADDIM, IDX_HEADDIM, IDX_HEADS)
_SRC = dict(zip(("pool_v", "pool_g", "z", "xbc", "dt", "q", "k", "v", "ag", "iq", "ik", "iw"),
                np.cumsum((0,) + _SRC_SIZES)[:-1].tolist()))
D_IN = sum(_SRC_SIZES)
PACK_ROWS_BLK = 128


def _pack_kernel(w_ref, o_ref):
    dt = o_ref.dtype
    rows = w_ref.shape[0]
    o_ref[:, 0:C_Q] = w_ref[:, 0:C_Q].astype(dt)
    for name, dst, width in (("q", C_Q, 1024), ("ag", C_AG, 1024), ("iq", C_IQ, 1024),
                             ("k", C_KV, 256), ("v", C_KV + 256, 256)):
        o_ref[:, dst:dst + width] = w_ref[:, _SRC[name]:_SRC[name] + width].astype(dt)
    misc = jnp.concatenate(
        [w_ref[:, _SRC["ik"]:_SRC["ik"] + IDX_HEADDIM], w_ref[:, _SRC["iw"]:_SRC["iw"] + IDX_HEADS],
         w_ref[:, _SRC["dt"]:_SRC["dt"] + SSD_HEADS],
         jnp.zeros((rows, LANES - IDX_HEADDIM - IDX_HEADS - SSD_HEADS), w_ref.dtype)], axis=1)
    o_ref[:, C_MISC:C_MISC + LANES] = misc.astype(dt)
    o_ref[:, N_USED:N_PACK] = jnp.zeros((rows, N_PACK - N_USED), dt)


def _pack_w_in(w_in):
    d = w_in.shape[0]
    return pl.pallas_call(
        _pack_kernel,
        grid=(d // PACK_ROWS_BLK,),
        in_specs=[pl.BlockSpec((PACK_ROWS_BLK, D_IN), lambda i: (i, 0))],
        out_specs=pl.BlockSpec((PACK_ROWS_BLK, N_PACK), lambda i: (i, 0)),
        out_shape=jax.ShapeDtypeStruct((d, N_PACK), MXU_DTYPE),
        compiler_params=pltpu.CompilerParams(
            dimension_semantics=("arbitrary",), vmem_limit_bytes=VMEM_LIMIT_BYTES),
        name="pack_w_in",
    )(w_in)


def _cast_kernel(x_ref, o_ref):
    o_ref[...] = x_ref[...].astype(o_ref.dtype)


def _cast_w_out(w_out):
    rows, cols = w_out.shape
    blk = 512
    return pl.pallas_call(
        _cast_kernel,
        grid=(rows // blk,),
        in_specs=[pl.BlockSpec((blk, cols), lambda i: (i, 0))],
        out_specs=pl.BlockSpec((blk, cols), lambda i: (i, 0)),
        out_shape=jax.ShapeDtypeStruct((rows, cols), MXU_DTYPE),
        compiler_params=pltpu.CompilerParams(
            dimension_semantics=("arbitrary",), vmem_limit_bytes=VMEM_LIMIT_BYTES),
        name="cast_w_out",
    )(w_out)


def _misc_row(vec):
    return jnp.zeros((1, LANES), f32).at[0, MISC_DT:MISC_DT + SSD_HEADS].set(vec.astype(f32))


def _layer(h, tabs, topk, n_real, pre_w, post_w, w_in, pool_w, pool_scale, conv_w, conv_b, dt_bias, a_log,
           d_skip, ssd_norm_w, w_out):
    t = _tiles(h.shape[0])
    proj = _inproj(h, pre_w[None, :], _pack_w_in(w_in), t["tm_in"], t["tn"])
    q_r, iq_r, kv_r, ik_a, ik_b = _rope(proj, tabs, t["tm"])
    mix_a = _mixer(proj, pool_w.astype(MXU_DTYPE), pool_scale[None, :], conv_w, conv_b[None, :],
                   _misc_row(dt_bias), _misc_row(a_log),
                   jnp.repeat(d_skip.astype(f32), SSD_HEADDIM)[None, :], ssd_norm_w[None, :])
    mix_c = _attention(proj, q_r, iq_r, kv_r, ik_a, ik_b, topk, n_real, t["tq"], t["tk"])
    return _outproj(mix_a, mix_c, _cast_w_out(w_out), h, post_w[None, :], t["tm"])


def _forward(x, meta_tokens, pre_norm_w, post_norm_w, w_in, pool_w, pool_scale, conv_w, conv_b,
             dt_bias, a_log, d_skip, ssd_norm_w, w_out):
    b, s, d = x.shape
    assert b == 1 and d == D_MODEL
    topk = min(INDEX_TOPK, s // 4)
    n_real = ROW0 + s
    tp = -(-n_real // ROW_ALIGN) * ROW_ALIGN
    h = jnp.concatenate([jnp.zeros((FRONT_PAD, d), x.dtype), meta_tokens.astype(x.dtype), x[0],
                         jnp.zeros((tp - n_real, d), x.dtype)], axis=0)
    tabs = _rope_tables(tp)
    for l in range(pre_norm_w.shape[0]):
        h = _layer(h, tabs, topk, n_real, pre_norm_w[l], post_norm_w[l], w_in[l], pool_w[l], pool_scale[l],
                   conv_w[l], conv_b[l], dt_bias[l], a_log[l], d_skip[l], ssd_norm_w[l], w_out[l])
    return h[ROW0:n_real][None]


def kernel(x, meta_tokens, pre_norm_w, post_norm_w, w_in, pool_w, pool_scale, conv_w, conv_b, dt_bias,
           a_log, d_skip, ssd_norm_w, w_out):
    return _forward(x, meta_tokens, pre_norm_w, post_norm_w, w_in, pool_w, pool_scale, conv_w, conv_b,
                    dt_bias, a_log, d_skip, ssd_norm_w, w_out)
```
